```python
import math
import jax, jax.numpy as jnp
from jax import lax
import numpy as np

D_MODEL = 1024
BATCH = 16
SEQ = 256
DEPTH = 4
DEC_BATCH = 8
DEC_SEQ = 2048
PAST_LEN = 512

GRID_W = 64
HEAD_DIM = 64
A_HEADS = 8
A_KV_HEADS = 2
B_HEADS = 8
C_HEADS = 8
NA_ROWS = 8
NA_COLS = 16
D_FF = 2816
N_MOD = 9
N_EVEN = (DEPTH + 1) // 2
N_ODD = DEPTH // 2
Q_BLOCK = 128
ROPE_THETA = 10000.0
NORM_EPS = 1e-6
MASK_VALUE = -1e30
AB_IN = (A_HEADS + 2 * A_KV_HEADS + 3 * B_HEADS) * HEAD_DIM
AB_OUT = (A_HEADS + B_HEADS) * HEAD_DIM
C_IN = 3 * C_HEADS * 2 * HEAD_DIM
C_OUT = C_HEADS * 2 * HEAD_DIM

kernel_name = 'hybrid_diffusion_prefix_trunk_step'


def rmsnorm(x, w):
    xf = x.astype(jnp.float32)
    y = xf * lax.rsqrt(jnp.mean(xf * xf, axis=-1, keepdims=True) + NORM_EPS)
    return (y * w.astype(jnp.float32)).astype(x.dtype)


def swiglu(h, w1, w3, w2):
    return (jax.nn.silu(h @ w1) * (h @ w3)) @ w2


def rope_tables(n_tokens, dim, dtype):
    t = jnp.arange(n_tokens)
    row = (t // GRID_W).astype(jnp.float32)
    col = (t % GRID_W).astype(jnp.float32)
    n_freq = dim // 4
    inv_freq = ROPE_THETA ** (-jnp.arange(n_freq, dtype=jnp.float32) / n_freq)
    ang_r = row[:, None] * inv_freq
    ang_c = col[:, None] * inv_freq
    ang = jnp.concatenate([ang_r, ang_r, ang_c, ang_c], axis=-1)
    return jnp.cos(ang).astype(dtype), jnp.sin(ang).astype(dtype)


def apply_axial_rope(x, cos, sin):
    S, Dh = cos.shape
    shape = (1, S) + (1,) * (x.ndim - 3) + (Dh,)
    cos = cos.reshape(shape)
    sin = sin.reshape(shape)
    x1r, x2r, x1c, x2c = jnp.split(x, 4, axis=-1)
    rot = jnp.concatenate([-x2r, x1r, -x2c, x1c], axis=-1)
    return x * cos + rot * sin


def map_query_blocks(fn, q):
    B, S = q.shape[0], q.shape[1]
    nb = S // Q_BLOCK
    qb = jnp.moveaxis(q.reshape((B, nb, Q_BLOCK) + q.shape[2:]), 1, 0)
    out = lax.map(fn, (jnp.arange(nb), qb))
    out = jnp.moveaxis(out, 0, 1)
    return out.reshape((B, S) + out.shape[3:])


def gqa_attention(q, k, v):
    B, S, Hq, Dh = q.shape
    Hkv = k.shape[2]
    G = Hq // Hkv
    Dv = v.shape[-1]
    scale = Dh ** -0.5

    def block(args):
        _, qb = args
        qg = qb.reshape(B, Q_BLOCK, Hkv, G, Dh)
        s = jnp.einsum('bqhgd,bkhd->bhgqk', qg, k).astype(jnp.float32) * scale
        p = jax.nn.softmax(s, axis=-1).astype(v.dtype)
        o = jnp.einsum('bhgqk,bkhd->bqhgd', p, v)
        return o.reshape(B, Q_BLOCK, Hq, Dv)

    return map_query_blocks(block, q)


def neighbourhood_attention(q, k, v, k_ctx, v_ctx, rpb):
    B, S, H, Dh = q.shape
    rows = S // GRID_W
    win_r = min(NA_ROWS, rows)
    win_c = min(NA_COLS, GRID_W)
    q_rows = Q_BLOCK // GRID_W
    band = min(win_r + q_rows - 1, rows)
    n_band = band * GRID_W
    scale = Dh ** -0.5
    k_grid = k.reshape(B, rows, GRID_W, H, Dh)
    v_grid = v.reshape(B, rows, GRID_W, H, v.shape[-1])
    q_local = jnp.arange(Q_BLOCK)
    q_col = q_local % GRID_W
    k_local = jnp.arange(n_band)
    k_col = k_local % GRID_W
    col_start = jnp.clip(q_col - win_c // 2, 0, GRID_W - win_c)
    col_in = (k_col[None, :] >= col_start[:, None]) & (k_col[None, :] < col_start[:, None] + win_c)
    dc = jnp.clip(k_col[None, :] - q_col[:, None] + NA_COLS - 1, 0, 2 * NA_COLS - 2)

    def block(args):
        j, qb = args
        q_row = j * q_rows + q_local // GRID_W
        row_start = jnp.clip(q_row - win_r // 2, 0, rows - win_r)
        band_start = jnp.clip(row_start[0], 0, rows - band)
        kb = lax.dynamic_slice_in_dim(k_grid, band_start, band, axis=1).reshape(B, n_band, H, Dh)
        vb = lax.dynamic_slice_in_dim(v_grid, band_start, band, axis=1).reshape(B, n_band, H, v.shape[-1])
        k_row = band_start + k_local // GRID_W
        row_in = (k_row[None, :] >= row_start[:, None]) & (k_row[None, :] < row_start[:, None] + win_r)
        dr = jnp.clip(k_row[None, :] - q_row[:, None] + NA_ROWS - 1, 0, 2 * NA_ROWS - 2)
        bias = rpb[:, dr, dc].astype(jnp.float32)
        s_nb = jnp.einsum('bqhd,bkhd->bhqk', qb, kb).astype(jnp.float32) * scale + bias
        s_nb = jnp.where(row_in & col_in, s_nb, MASK_VALUE)
        s_ctx = jnp.einsum('bqhd,bkhd->bhqk', qb, k_ctx).astype(jnp.float32) * scale
        p = jax.nn.softmax(jnp.concatenate([s_nb, s_ctx], axis=-1), axis=-1).astype(v.dtype)
        return (jnp.einsum('bhqk,bkhd->bqhd', p[..., :n_band], vb)
                + jnp.einsum('bhqk,bkhd->bqhd', p[..., n_band:], v_ctx))

    return map_query_blocks(block, q)


def diff_attention(q, k, v, lam):
    Dh = q.shape[-1]
    scale = Dh ** -0.5

    def block(args):
        _, qb = args
        s = jnp.einsum('bqhjd,bkhjd->bhjqk', qb, k).astype(jnp.float32) * scale
        p = jax.nn.softmax(s, axis=-1)
        a = (p[:, :, 0] - lam * p[:, :, 1]).astype(v.dtype)
        return jnp.einsum('bhqk,bkhd->bqhd', a, v)

    return map_query_blocks(block, q)


def diff_lambda(lam_params, lam_init):
    lp = lam_params.astype(jnp.float32)
    return jnp.exp(jnp.sum(lp[0] * lp[1])) - jnp.exp(jnp.sum(lp[2] * lp[3])) + lam_init


def even_project(h, w_in, q_norm, k_norm):
    B, S, _ = h.shape
    sizes = [A_HEADS * HEAD_DIM, A_KV_HEADS * HEAD_DIM, A_KV_HEADS * HEAD_DIM,
             B_HEADS * HEAD_DIM, B_HEADS * HEAD_DIM]
    splits = [sum(sizes[:i + 1]) for i in range(len(sizes))]
    aq, ak, av, bq, bk, bv = jnp.split(h @ w_in, splits, axis=-1)
    aq = rmsnorm(aq.reshape(B, S, A_HEADS, HEAD_DIM), q_norm)
    ak = rmsnorm(ak.reshape(B, S, A_KV_HEADS, HEAD_DIM), k_norm)
    av = av.reshape(B, S, A_KV_HEADS, HEAD_DIM)
    bq = bq.reshape(B, S, B_HEADS, HEAD_DIM)
    bk = bk.reshape(B, S, B_HEADS, HEAD_DIM)
    bv = bv.reshape(B, S, B_HEADS, HEAD_DIM)
    return aq, ak, av, bq, bk, bv


def even_merge(oa, ob, w_out):
    B, S = oa.shape[0], oa.shape[1]
    return jnp.concatenate([oa.reshape(B, S, -1), ob.reshape(B, S, -1)], axis=-1) @ w_out


def even_mixer_ctx(h, w_in, w_out, q_norm, k_norm):
    aq, ak, av, bq, bk, bv = even_project(h, w_in, q_norm, k_norm)
    oa = gqa_attention(aq, ak, av)
    ob = gqa_attention(bq, bk, bv)
    return even_merge(oa, ob, w_out), (ak, av, bk, bv)


def even_mixer_lat(h, w_in, w_out, q_norm, k_norm, rpb, cos, sin, ak_c, av_c, bk_c, bv_c):
    aq, ak, av, bq, bk, bv = even_project(h, w_in, q_norm, k_norm)
    aq = apply_axial_rope(aq, cos, sin)
    ak = apply_axial_rope(ak, cos, sin)
    oa = gqa_attention(aq, jnp.concatenate([ak, ak_c], axis=1), jnp.concatenate([av, av_c], axis=1))
    ob = neighbourhood_attention(bq, bk, bv, bk_c, bv_c, rpb)
    return even_merge(oa, ob, w_out), ()


def odd_project(h, w_in):
    B, S, _ = h.shape
    q, k, v = jnp.split(h @ w_in, 3, axis=-1)
    return (q.reshape(B, S, C_HEADS, 2, HEAD_DIM), k.reshape(B, S, C_HEADS, 2, HEAD_DIM),
            v.reshape(B, S, C_HEADS, 2 * HEAD_DIM))


def odd_output(o, subln, lam_init, w_out):
    B, S = o.shape[0], o.shape[1]
    o = rmsnorm(o, subln) * (1.0 - lam_init)
    return o.reshape(B, S, C_OUT) @ w_out


def odd_mixer_ctx(h, w_in, w_out, lam_params, subln, lam_init):
    B, S, _ = h.shape
    q, k, v = odd_project(h, w_in)
    o = diff_attention(q, k, v, diff_lambda(lam_params, lam_init))
    return odd_output(o, subln, lam_init, w_out), (k.reshape(B, S, C_HEADS, 2 * HEAD_DIM), v)


def odd_mixer_lat(h, w_in, w_out, lam_params, subln, lam_init, cos, sin, k_c, v_c):
    B, _, _ = h.shape
    q, k, v = odd_project(h, w_in)
    q = apply_axial_rope(q, cos, sin)
    k = apply_axial_rope(k, cos, sin)
    k_c = k_c.reshape(B, k_c.shape[1], C_HEADS, 2, HEAD_DIM)
    o = diff_attention(q, jnp.concatenate([k, k_c], axis=1), jnp.concatenate([v, v_c], axis=1),
                       diff_lambda(lam_params, lam_init))
    return odd_output(o, subln, lam_init, w_out), ()


def macaron_layer(x, mod, norm_w, w1, w3, w2, mixer):
    m = [mod[:, i][:, None, :] for i in range(N_MOD)]
    h = rmsnorm(x, norm_w[0]) * (1.0 + m[1]) + m[0]
    x = x + 0.5 * m[2] * swiglu(h, w1[0], w3[0], w2[0])
    h = rmsnorm(x, norm_w[1]) * (1.0 + m[4]) + m[3]
    mix_out, extras = mixer(h)
    x = x + m[5] * mix_out
    h = rmsnorm(x, norm_w[2]) * (1.0 + m[7]) + m[6]
    x = x + 0.5 * m[8] * swiglu(h, w1[1], w3[1], w2[1])
    return x, extras


def setup_inputs(seed: int = 0) -> dict:
    key = jax.random.key(seed)
    ks = jax.random.split(key, 32)
    f32 = jnp.float32

    def nrm(k, shape, scale=1.0):
        return jax.random.normal(k, shape, f32) * scale

    return {
        'x_prompt': nrm(ks[0], (BATCH, SEQ, D_MODEL)),
        'x_sample': nrm(ks[1], (DEC_BATCH, DEC_SEQ, D_MODEL)),
        'cache_a_k': nrm(ks[2], (DEC_BATCH, N_EVEN, PAST_LEN, A_KV_HEADS, HEAD_DIM)),
        'cache_a_v': nrm(ks[3], (DEC_BATCH, N_EVEN, PAST_LEN, A_KV_HEADS, HEAD_DIM)),
        'cache_b_k': nrm(ks[4], (DEC_BATCH, N_EVEN, PAST_LEN, B_HEADS, HEAD_DIM)),
        'cache_b_v': nrm(ks[5], (DEC_BATCH, N_EVEN, PAST_LEN, B_HEADS, HEAD_DIM)),
        'cache_c_k': nrm(ks[6], (DEC_BATCH, N_ODD, PAST_LEN, C_HEADS, 2 * HEAD_DIM)),
        'cache_c_v': nrm(ks[7], (DEC_BATCH, N_ODD, PAST_LEN, C_HEADS, 2 * HEAD_DIM)),
        'c': nrm(ks[8], (DEC_BATCH, D_MODEL)),
        'c_ctx': nrm(ks[9], (D_MODEL,)),
        'w_mod': nrm(ks[10], (DEPTH, D_MODEL, N_MOD * D_MODEL), 0.5 * D_MODEL ** -0.5),
        'b_mod': nrm(ks[11], (DEPTH, N_MOD * D_MODEL), 0.01),
        'norm_w': 1.0 + nrm(ks[12], (DEPTH, 3, D_MODEL), 0.02),
        'ffn_w1': nrm(ks[13], (DEPTH, 2, D_MODEL, D_FF), D_MODEL ** -0.5),
        'ffn_w3': nrm(ks[14], (DEPTH, 2, D_MODEL, D_FF), D_MODEL ** -0.5),
        'ffn_w2': nrm(ks[15], (DEPTH, 2, D_FF, D_MODEL), D_FF ** -0.5),
        'w_in_ab': nrm(ks[16], (N_EVEN, D_MODEL, AB_IN), D_MODEL ** -0.5),
        'w_out_ab': nrm(ks[17], (N_EVEN, AB_OUT, D_MODEL), AB_OUT ** -0.5),
        'a_q_norm': 1.0 + nrm(ks[18], (N_EVEN, HEAD_DIM), 0.02),
        'a_k_norm': 1.0 + nrm(ks[19], (N_EVEN, HEAD_DIM), 0.02),
        'b_rpb': nrm(ks[20], (N_EVEN, B_HEADS, 2 * NA_ROWS - 1, 2 * NA_COLS - 1), 0.1),
        'w_in_c': nrm(ks[21], (N_ODD, D_MODEL, C_IN), D_MODEL ** -0.5),
        'w_out_c': nrm(ks[22], (N_ODD, C_OUT, D_MODEL), C_OUT ** -0.5),
        'c_lambda': nrm(ks[23], (N_ODD, 4, HEAD_DIM), 0.1),
        'c_subln': 1.0 + nrm(ks[24], (N_ODD, 2 * HEAD_DIM), 0.02),
        'final_norm': 1.0 + nrm(ks[25], (D_MODEL,), 0.02),
    }


def reference(x_prompt, x_sample, cache_a_k, cache_a_v, cache_b_k, cache_b_v, cache_c_k, cache_c_v,
              c, c_ctx, w_mod, b_mod, norm_w, ffn_w1, ffn_w3, ffn_w2, w_in_ab, w_out_ab,
              a_q_norm, a_k_norm, b_rpb, w_in_c, w_out_c, c_lambda, c_subln, final_norm):
    lam_inits = [0.8 - 0.6 * math.exp(-0.3 * l) for l in range(DEPTH)]

    xp = x_prompt
    ak_l, av_l, bk_l, bv_l, ck_l, cv_l = [], [], [], [], [], []
    for l in range(DEPTH):
        mod = (jax.nn.silu(c_ctx) @ w_mod[l] + b_mod[l]).reshape(1, N_MOD, D_MODEL)
        if l % 2 == 0:
            e = l // 2
            mixer = lambda h: even_mixer_ctx(h, w_in_ab[e], w_out_ab[e], a_q_norm[e], a_k_norm[e])
            xp, (ak, av, bk, bv) = macaron_layer(xp, mod, norm_w[l], ffn_w1[l], ffn_w3[l], ffn_w2[l], mixer)
            ak_l.append(ak); av_l.append(av); bk_l.append(bk); bv_l.append(bv)
        else:
            o = l // 2
            mixer = lambda h: odd_mixer_ctx(h, w_in_c[o], w_out_c[o], c_lambda[o], c_subln[o], lam_inits[l])
            xp, (ck, cv) = macaron_layer(xp, mod, norm_w[l], ffn_w1[l], ffn_w3[l], ffn_w2[l], mixer)
            ck_l.append(ck); cv_l.append(cv)
    y_prompt = rmsnorm(xp, final_norm)
    new_a_k = jnp.stack(ak_l, axis=1)
    new_a_v = jnp.stack(av_l, axis=1)
    new_b_k = jnp.stack(bk_l, axis=1)
    new_b_v = jnp.stack(bv_l, axis=1)
    new_c_k = jnp.stack(ck_l, axis=1)
    new_c_v = jnp.stack(cv_l, axis=1)

    xs = x_sample
    cos, sin = rope_tables(x_sample.shape[1], HEAD_DIM, x_sample.dtype)
    for l in range(DEPTH):
        mod = (jax.nn.silu(c) @ w_mod[l] + b_mod[l]).reshape(-1, N_MOD, D_MODEL)
        if l % 2 == 0:
            e = l // 2
            mixer = lambda h: even_mixer_lat(h, w_in_ab[e], w_out_ab[e], a_q_norm[e], a_k_norm[e], b_rpb[e],
                                             cos, sin, cache_a_k[:, e], cache_a_v[:, e],
                                             cache_b_k[:, e], cache_b_v[:, e])
        else:
            o = l // 2
            mixer = lambda h: odd_mixer_lat(h, w_in_c[o], w_out_c[o], c_lambda[o], c_subln[o], lam_inits[l],
                                            cos, sin, cache_c_k[:, o], cache_c_v[:, o])
        xs, _ = macaron_layer(xs, mod, norm_w[l], ffn_w1[l], ffn_w3[l], ffn_w2[l], mixer)
    y_sample = rmsnorm(xs, final_norm)

    return (y_prompt, y_sample, new_a_k, new_a_v, new_b_k, new_b_v, new_c_k, new_c_v)
```

```python
import functools
import math

import numpy as np
import jax
import jax.numpy as jnp
from jax import lax
from jax.experimental import pallas as pl
from jax.experimental.pallas import tpu as pltpu

D_MODEL = 1024
BATCH = 16
SEQ = 256
DEPTH = 4
DEC_BATCH = 8
DEC_SEQ = 2048
PAST_LEN = 512
GRID_W = 64
GRID_ROWS = DEC_SEQ // GRID_W
HEAD_DIM = 64
A_HEADS = 8
A_KV_HEADS = 2
B_HEADS = 8
C_HEADS = 8
NA_ROWS = 8
NA_COLS = 16
D_FF = 2816
N_MOD = 9
ROPE_THETA = 10000.0
NORM_EPS = 1e-6
MASK_VALUE = -1e30
AB_IN = (A_HEADS + 2 * A_KV_HEADS + 3 * B_HEADS) * HEAD_DIM
C_IN = 3 * C_HEADS * 2 * HEAD_DIM

LANES = 128
FF_CHUNK = 256
N_FF_CHUNKS = D_FF // FF_CHUNK
MOD_ROWS = 16
CTX_MOD_ROW = DEC_BATCH
VMEM_LIMIT = 52 * 1024 * 1024

N_LAT = DEC_BATCH * DEC_SEQ
N_CTX = BATCH * SEQ

NB_Q_ROWS = 2
NB_Q = NB_Q_ROWS * GRID_W
NB_BAND_ROWS = 10
NB_BAND = NB_BAND_ROWS * GRID_W
NB_BLOCKS = DEC_SEQ // NB_Q
NB_TABLES = 5
RPB_SIZE = (2 * NA_ROWS - 1) * (2 * NA_COLS - 1)
RPB_PAD = 512

bf16 = jnp.bfloat16
f32 = jnp.float32


def _dot(a, b):
    return jnp.dot(a, b, preferred_element_type=f32)


def _dot_nt(a, b):
    return lax.dot_general(a, b, (((1,), (1,)), ((), ())), preferred_element_type=f32)


def _cparams(n_axes):
    return pltpu.CompilerParams(dimension_semantics=("arbitrary",) * n_axes,
                                vmem_limit_bytes=VMEM_LIMIT)


def _const_spec(shape):
    nd = len(shape)
    return pl.BlockSpec(shape, lambda *_: (0,) * nd, pipeline_mode=pl.Buffered(1))


def _modulated_norm(x, mod_ref, nw_ref):
    ms = jnp.mean(x * x, axis=-1, keepdims=True)
    y = x * lax.rsqrt(ms + NORM_EPS) * nw_ref[...]
    return y * (1.0 + mod_ref[1:2, :]) + mod_ref[0:1, :]


def _mod_kernel(c_ref, w_ref, b_ref, o_ref):
    c = c_ref[...]
    s = (c * jax.nn.sigmoid(c)).astype(bf16)
    o_ref[...] = _dot(s, w_ref[...].astype(bf16)) + b_ref[...]


def _modulation(c, c_ctx, w_mod, b_mod):
    cc = jnp.zeros((MOD_ROWS, D_MODEL), f32)
    cc = cc.at[:DEC_BATCH].set(c).at[CTX_MOD_ROW].set(c_ctx)
    out = pl.pallas_call(
        _mod_kernel,
        grid=(DEPTH, N_MOD),
        in_specs=[
            pl.BlockSpec((MOD_ROWS, D_MODEL), lambda l, k: (0, 0)),
            pl.BlockSpec((None, D_MODEL, D_MODEL), lambda l, k: (l, 0, k)),
            pl.BlockSpec((None, None, 1, D_MODEL), lambda l, k: (l, k, 0, 0)),
        ],
        out_specs=pl.BlockSpec((None, None, MOD_ROWS, D_MODEL), lambda l, k: (l, k, 0, 0)),
        out_shape=jax.ShapeDtypeStruct((DEPTH, N_MOD, MOD_ROWS, D_MODEL), f32),
        compiler_params=_cparams(2),
    )(cc, w_mod, b_mod.reshape(DEPTH, N_MOD, 1, D_MODEL))
    return out.reshape(DEPTH, 3, 3, MOD_ROWS, D_MODEL).transpose(0, 1, 3, 2, 4)


def _mod_spec(layer, sub, rows_per_seq):
    if rows_per_seq is None:
        return pl.BlockSpec((None, None, None, 3, D_MODEL),
                            lambda i, *_: (layer, sub, CTX_MOD_ROW, 0, 0))
    return pl.BlockSpec((None, None, None, 3, D_MODEL),
                        lambda i, *_: (layer, sub, i // rows_per_seq, 0, 0))


def _ffn_kernel(x_ref, mod_ref, nw_ref, w1_ref, w3_ref, w2_ref, o_ref, acc_ref):
    x = x_ref[...]
    h = _modulated_norm(x, mod_ref, nw_ref).astype(bf16)
    acc_ref[...] = jnp.zeros_like(acc_ref)

    def body(ci, carry):
        a1 = _dot(h, w1_ref[ci])
        a3 = _dot(h, w3_ref[ci])
        g = (a1 * jax.nn.sigmoid(a1) * a3).astype(bf16)
        acc_ref[...] += _dot(g, w2_ref[ci])
        return carry

    lax.fori_loop(0, N_FF_CHUNKS, body, 0)
    o_ref[...] = x + 0.5 * mod_ref[2:3, :] * acc_ref[...]


def _ffn(x, mod, layer, sub, nw, w1, w3, w2, tm, seq_len):
    n = x.shape[0]
    tiles_per_seq = None if seq_len is None else seq_len // tm
    return pl.pallas_call(
        _ffn_kernel,
        grid=(n // tm,),
        in_specs=[
            pl.BlockSpec((tm, D_MODEL), lambda i: (i, 0)),
            _mod_spec(layer, sub, tiles_per_seq),
            _const_spec((1, D_MODEL)),
            _const_spec((N_FF_CHUNKS, D_MODEL, FF_CHUNK)),
            _const_spec((N_FF_CHUNKS, D_MODEL, FF_CHUNK)),
            _const_spec((N_FF_CHUNKS, FF_CHUNK, D_MODEL)),
        ],
        out_specs=pl.BlockSpec((tm, D_MODEL), lambda i: (i, 0)),
        out_shape=jax.ShapeDtypeStruct((n, D_MODEL), f32),
        scratch_shapes=[pltpu.VMEM((tm, D_MODEL), f32)],
        compiler_params=_cparams(1),
    )(x, mod, nw, w1, w3, w2)


def _lane_iota():
    return lax.broadcasted_iota(jnp.int32, (1, LANES), 1)


def _head_rms(z, g_ref):
    sq = z * z
    hi = sq.astype(bf16)
    lo = (sq - hi.astype(f32)).astype(bf16)
    return _dot(jnp.concatenate([hi, lo], axis=1), g_ref[...])


def _rope(z, cos, sin_signed, first_half):
    rot = jnp.where(first_half, pltpu.roll(z, LANES - 16, 1), pltpu.roll(z, 16, 1))
    return z * cos + rot * sin_signed


def _segment_sum_matrix():
    blk = np.kron(np.eye(2, dtype=np.float32), np.ones((HEAD_DIM, HEAD_DIM), np.float32)) / HEAD_DIM
    return jnp.asarray(np.concatenate([blk, blk], axis=0), dtype=bf16)


def _proj_even_kernel(*refs, rope, ctx):
    x_ref, mod_ref, nw_ref, w_ref, qn_ref, kn_ref, g_ref = refs[:7]
    pos = 7
    if rope:
        cos_ref, sin_ref = refs[7:9]
        pos = 9
    aq_ref, ak_ref, av_ref, bq_ref, bk_ref, bv_ref = refs[pos:pos + 6]
    if ctx:
        akf_ref, avf_ref, bkf_ref, bvf_ref = refs[pos + 6:pos + 10]

    h = _modulated_norm(x_ref[...], mod_ref, nw_ref).astype(bf16)
    y = _dot(h, w_ref[...])
    lane = _lane_iota()
    low = lane < HEAD_DIM
    first_half = (lane % 32) < 16
    scale = HEAD_DIM ** -0.5
    if rope:
        cos = cos_ref[...]
        sin = sin_ref[...]

    for j in range(A_HEADS // 2):
        z = y[:, j * LANES:(j + 1) * LANES]
        z = z * lax.rsqrt(_head_rms(z, g_ref) + NORM_EPS) * qn_ref[...]
        if rope:
            z = _rope(z, cos, sin, first_half)
        z = z * scale
        zr = pltpu.roll(z, HEAD_DIM, 1)
        g = j // 2
        keep = low if g == 0 else jnp.logical_not(low)
        h0 = z if g == 0 else zr
        h1 = zr if g == 0 else z
        aq_ref[:, (2 * j) * LANES:(2 * j + 1) * LANES] = jnp.where(keep, h0, 0.0).astype(bf16)
        aq_ref[:, (2 * j + 1) * LANES:(2 * j + 2) * LANES] = jnp.where(keep, h1, 0.0).astype(bf16)

    o = A_HEADS * HEAD_DIM
    z = y[:, o:o + LANES]
    z = z * lax.rsqrt(_head_rms(z, g_ref) + NORM_EPS) * kn_ref[...]
    if ctx:
        akf_ref[...] = z
    if rope:
        z = _rope(z, cos, sin, first_half)
    ak_ref[...] = z.astype(bf16)
    o += LANES
    z = y[:, o:o + LANES]
    if ctx:
        avf_ref[...] = z
    av_ref[...] = z.astype(bf16)
    o += LANES
    nb = B_HEADS * HEAD_DIM
    bq_ref[...] = (y[:, o:o + nb] * scale).astype(bf16)
    z = y[:, o + nb:o + 2 * nb]
    if ctx:
        bkf_ref[...] = z
    bk_ref[...] = z.astype(bf16)
    z = y[:, o + 2 * nb:o + 3 * nb]
    if ctx:
        bvf_ref[...] = z
    bv_ref[...] = z.astype(bf16)


def _proj_even(x, mod, layer, nw, w_in, qn, kn, gmat, rope_tabs, tm, ctx):
    n = x.shape[0]
    rope = rope_tabs is not None
    tiles_per_seq = None if ctx else DEC_SEQ // tm
    row = lambda i: (i, 0)
    in_specs = [
        pl.BlockSpec((tm, D_MODEL), row),
        _mod_spec(layer, 1, tiles_per_seq),
        _const_spec((1, D_MODEL)),
        _const_spec((D_MODEL, AB_IN)),
        _const_spec((1, LANES)),
        _const_spec((1, LANES)),
        _const_spec((2 * LANES, LANES)),
    ]
    args = [x, mod, nw, w_in, qn, kn, gmat]
    if rope:
        pos_spec = pl.BlockSpec((tm, LANES), lambda i: (i % tiles_per_seq, 0))
        in_specs += [pos_spec, pos_spec]
        args += list(rope_tabs)
    widths = [2 * A_HEADS * HEAD_DIM, LANES, LANES, 512, 512, 512]
    out_shape = [jax.ShapeDtypeStruct((n, w), bf16) for w in widths]
    out_specs = [pl.BlockSpec((tm, w), row) for w in widths]
    if ctx:
        for w in (LANES, LANES, 512, 512):
            out_shape.append(jax.ShapeDtypeStruct((n, w), f32))
            out_specs.append(pl.BlockSpec((tm, w), row))
    return pl.pallas_call(
        functools.partial(_proj_even_kernel, rope=rope, ctx=ctx),
        grid=(n // tm,),
        in_specs=in_specs,
        out_specs=out_specs,
        out_shape=out_shape,
        compiler_params=_cparams(1),
    )(*args)


def _proj_odd_kernel(*refs, rope, ctx):
    x_ref, mod_ref, nw_ref, w_ref = refs[:4]
    pos = 4
    if rope:
        cos_ref, sin_ref = refs[4:6]
        pos = 6
    q_ref, k_ref, v_ref = refs[pos:pos + 3]
    if ctx:
        kf_ref, vf_ref = refs[pos + 3:pos + 5]

    h = _modulated_norm(x_ref[...], mod_ref, nw_ref).astype(bf16)
    y = _dot(h, w_ref[...])
    lane = _lane_iota()
    first_half = (lane % 32) < 16
    scale = HEAD_DIM ** -0.5
    width = C_HEADS * 2 * HEAD_DIM
    for j in range(width // LANES):
        zq = y[:, j * LANES:(j + 1) * LANES]
        zk = y[:, width + j * LANES:width + (j + 1) * LANES]
        if ctx:
            kf_ref[:, j * LANES:(j + 1) * LANES] = zk
        if rope:
            zq = _rope(zq, cos_ref[...], sin_ref[...], first_half)
            zk = _rope(zk, cos_ref[...], sin_ref[...], first_half)
        q_ref[:, j * LANES:(j + 1) * LANES] = (zq * scale).astype(bf16)
        k_ref[:, j * LANES:(j + 1) * LANES] = zk.astype(bf16)
    zv = y[:, 2 * width:3 * width]
    if ctx:
        vf_ref[...] = zv
    v_ref[...] = zv.astype(bf16)


def _proj_odd(x, mod, layer, nw, w_in, rope_tabs, tm, ctx):
    n = x.shape[0]
    rope = rope_tabs is not None
    tiles_per_seq = None if ctx else DEC_SEQ // tm
    row = lambda i: (i, 0)
    width = C_HEADS * 2 * HEAD_DIM
    in_specs = [
        pl.BlockSpec((tm, D_MODEL), row),
        _mod_spec(layer, 1, tiles_per_seq),
        _const_spec((1, D_MODEL)),
        _const_spec((D_MODEL, C_IN)),
    ]
    args = [x, mod, nw, w_in]
    if rope:
        pos_spec = pl.BlockSpec((tm, LANES), lambda i: (i % tiles_per_seq, 0))
        in_specs += [pos_spec, pos_spec]
        args += list(rope_tabs)
    out_shape = [jax.ShapeDtypeStruct((n, width), bf16)] * 3
    out_specs = [pl.BlockSpec((tm, width), row)] * 3
    if ctx:
        out_shape = out_shape + [jax.ShapeDtypeStruct((n, width), f32)] * 2
        out_specs = out_specs + [pl.BlockSpec((tm, width), row)] * 2
    return pl.pallas_call(
        functools.partial(_proj_odd_kernel, rope=rope, ctx=ctx),
        grid=(n // tm,),
        in_specs=in_specs,
        out_specs=out_specs,
        out_shape=out_shape,
        compiler_params=_cparams(1),
    )(*args)


def _softmax_parts(parts):
    m = parts[0].max(axis=-1, keepdims=True)
    for s in parts[1:]:
        m = jnp.maximum(m, s.max(axis=-1, keepdims=True))
    ps = [jnp.exp(s - m) for s in parts]
    l = ps[0].sum(axis=-1, keepdims=True)
    for p in ps[1:]:
        l = l + p.sum(axis=-1, keepdims=True)
    return ps, l


def _attn_a_kernel(*refs, tq, cache):
    if cache:
        q_ref, k_ref, v_ref, kc_ref, vc_ref, o_ref = refs
    else:
        q_ref, k_ref, v_ref, o_ref = refs
    low = _lane_iota() < HEAD_DIM
    group = A_HEADS // A_KV_HEADS
    ks = [k_ref[...]]
    vs = [v_ref[...]]
    if cache:
        ks.append(kc_ref[...].astype(bf16))
        vs.append(vc_ref[...].astype(bf16))
    for g in range(A_KV_HEADS):
        q = jnp.concatenate(
            [q_ref[:, (group * g + i) * LANES:(group * g + i + 1) * LANES] for i in range(group)], axis=0)
        ps, l = _softmax_parts([_dot_nt(q, k) for k in ks])
        o = _dot(ps[0].astype(bf16), vs[0])
        for p, v in zip(ps[1:], vs[1:]):
            o = o + _dot(p.astype(bf16), v)
        o = o / l
        if g == 1:
            o_roll = o
            o = pltpu.roll(o, HEAD_DIM, 1)
        else:
            o_roll = pltpu.roll(o, HEAD_DIM, 1)
        for jj in range(group // 2):
            even = o[(2 * jj) * tq:(2 * jj + 1) * tq]
            odd = o_roll[(2 * jj + 1) * tq:(2 * jj + 2) * tq]
            blk = (group // 2) * g + jj
            o_ref[:, blk * LANES:(blk + 1) * LANES] = jnp.where(low, even, odd).astype(o_ref.dtype)


def _cache_spec(layer, width):
    return pl.BlockSpec((None, None, PAST_LEN, width), lambda b, j: (b, layer, 0, 0))


def _attn_a(q, k, v, kc, vc, layer, batch, seq, tq):
    cache = kc is not None
    nq = seq // tq
    in_specs = [
        pl.BlockSpec((tq, 2 * A_HEADS * HEAD_DIM), lambda b, j: (b * nq + j, 0)),
        pl.BlockSpec((seq, LANES), lambda b, j: (b, 0)),
        pl.BlockSpec((seq, LANES), lambda b, j: (b, 0)),
    ]
    args = [q, k, v]
    if cache:
        in_specs += [_cache_spec(layer, LANES)] * 2
        args += [kc, vc]
    return pl.pallas_call(
        functools.partial(_attn_a_kernel, tq=tq, cache=cache),
        grid=(batch, nq),
        in_specs=in_specs,
        out_specs=pl.BlockSpec((tq, A_HEADS * HEAD_DIM), lambda b, j: (b * nq + j, 0)),
        out_shape=jax.ShapeDtypeStruct((batch * seq, A_HEADS * HEAD_DIM), bf16),
        compiler_params=_cparams(2),
    )(*args)


def _pair_lhs(q, low):
    zero = jnp.zeros_like(q)
    return jnp.concatenate([jnp.where(low, q, zero), jnp.where(low, zero, q)], axis=0)


def _attn_pair_ctx_kernel(q_ref, k_ref, v_ref, o_ref, *, tq):
    low = _lane_iota() < HEAD_DIM
    for hp in range(B_HEADS // 2):
        sl = slice(hp * LANES, (hp + 1) * LANES)
        q = _pair_lhs(q_ref[:, sl], low)
        ps, l = _softmax_parts([_dot_nt(q, k_ref[:, sl])])
        o = _dot(ps[0].astype(bf16), v_ref[:, sl]) / l
        o_ref[:, sl] = jnp.where(low, o[:tq], o[tq:]).astype(o_ref.dtype)


def _attn_pair_ctx(q, k, v, batch, seq):
    spec = pl.BlockSpec((seq, B_HEADS * HEAD_DIM), lambda b: (b, 0))
    return pl.pallas_call(
        functools.partial(_attn_pair_ctx_kernel, tq=seq),
        grid=(batch,),
        in_specs=[spec, spec, spec],
        out_specs=spec,
        out_shape=jax.ShapeDtypeStruct((batch * seq, B_HEADS * HEAD_DIM), bf16),
        compiler_params=_cparams(1),
    )(q, k, v)


def _nb_band_start(j):
    return jnp.clip(NB_Q_ROWS * j - NA_ROWS // 2, 0, GRID_ROWS - NB_BAND_ROWS)


def _nb_table_index(j):
    return jnp.where(j < 2, j, jnp.where(j >= NB_BLOCKS - 2, j - (NB_BLOCKS - NB_TABLES), 2))


def _attn_nb_kernel(q_ref, k_ref, v_ref, kc_ref, vc_ref, bias_ref, o_ref):
    j = pl.program_id(1)
    start = pl.multiple_of(_nb_band_start(j) * GRID_W, GRID_W)
    low = _lane_iota() < HEAD_DIM
    for hp in range(B_HEADS // 2):
        sl = slice(hp * LANES, (hp + 1) * LANES)
        q = _pair_lhs(q_ref[:, sl], low)
        kb = k_ref[pl.ds(start, NB_BAND), sl]
        vb = v_ref[pl.ds(start, NB_BAND), sl]
        bias = jnp.concatenate([bias_ref[2 * hp], bias_ref[2 * hp + 1]], axis=0)
        s_nb = _dot_nt(q, kb) + bias
        s_ctx = _dot_nt(q, kc_ref[:, sl].astype(bf16))
        ps, l = _softmax_parts([s_nb, s_ctx])
        o = (_dot(ps[0].astype(bf16), vb) + _dot(ps[1].astype(bf16), vc_ref[:, sl].astype(bf16))) / l
        o_ref[:, sl] = jnp.where(low, o[:NB_Q], o[NB_Q:]).astype(o_ref.dtype)


def _attn_nb(q, k, v, kc, vc, bias, layer):
    width = B_HEADS * HEAD_DIM
    return pl.pallas_call(
        _attn_nb_kernel,
        grid=(DEC_BATCH, NB_BLOCKS),
        in_specs=[
            pl.BlockSpec((NB_Q, width), lambda b, j: (b * NB_BLOCKS + j, 0)),
            pl.BlockSpec((DEC_SEQ, width), lambda b, j: (b, 0)),
            pl.BlockSpec((DEC_SEQ, width), lambda b, j: (b, 0)),
            _cache_spec(layer, width),
            _cache_spec(layer, width),
            pl.BlockSpec((B_HEADS, None, NB_Q, NB_BAND), lambda b, j: (layer, _nb_table_index(j), 0, 0)),
        ],
        out_specs=pl.BlockSpec((NB_Q, width), lambda b, j: (b * NB_BLOCKS + j, 0)),
        out_shape=jax.ShapeDtypeStruct((N_LAT, width), bf16),
        compiler_params=_cparams(2),
    )(q, k, v, kc, vc, bias)


def _nb_bias_index():
    idx = np.full((NB_TABLES, NB_Q, NB_BAND), -1, np.int32)
    blocks = [0, 1, 2, NB_BLOCKS - 2, NB_BLOCKS - 1]
    win_c = min(NA_COLS, GRID_W)
    for t, j in enumerate(blocks):
        band_start = int(np.clip(NB_Q_ROWS * j - NA_ROWS // 2, 0, GRID_ROWS - NB_BAND_ROWS))
        ql = np.arange(NB_Q)
        q_row = j * NB_Q_ROWS + ql // GRID_W
        q_col = ql % GRID_W
        kl = np.arange(NB_BAND)
        k_row = band_start + kl // GRID_W
        k_col = kl % GRID_W
        row_start = np.clip(q_row - NA_ROWS // 2, 0, GRID_ROWS - NA_ROWS)
        col_start = np.clip(q_col - win_c // 2, 0, GRID_W - win_c)
        row_in = (k_row[None] >= row_start[:, None]) & (k_row[None] < row_start[:, None] + NA_ROWS)
        col_in = (k_col[None] >= col_start[:, None]) & (k_col[None] < col_start[:, None] + win_c)
        dr = np.clip(k_row[None] - q_row[:, None] + NA_ROWS - 1, 0, 2 * NA_ROWS - 2)
        dc = np.clip(k_col[None] - q_col[:, None] + NA_COLS - 1, 0, 2 * NA_COLS - 2)
        flat = dr * (2 * NA_COLS - 1) + dc
        idx[t] = np.where(row_in & col_in, flat, -1)
    return idx.reshape(1, -1)


def _nb_bias_kernel(rpb_ref, idx_ref, o_ref):
    r = rpb_ref[...]
    hi = r.astype(bf16)
    r1 = r - hi.astype(f32)
    mid = r1.astype(bf16)
    lo = (r1 - mid.astype(f32)).astype(bf16)
    idx = idx_ref[...]
    onehot = (lax.broadcasted_iota(jnp.int32, (RPB_PAD, idx.shape[1]), 0) == idx)
    onehot = jnp.where(onehot, 1.0, 0.0).astype(bf16)
    val = (_dot(lo, onehot) + _dot(mid, onehot)) + _dot(hi, onehot)
    o_ref[...] = jnp.where(idx >= 0, val, MASK_VALUE)


def _nb_bias_tables(b_rpb):
    n_even = b_rpb.shape[0]
    rows = n_even * B_HEADS
    rpb = jnp.pad(b_rpb.reshape(rows, RPB_SIZE), ((0, 0), (0, RPB_PAD - RPB_SIZE)))
    idx = jnp.asarray(_nb_bias_index())
    n_pos = idx.shape[1]
    chunk = 2048
    out = pl.pallas_call(
        _nb_bias_kernel,
        grid=(n_pos // chunk,),
        in_specs=[
            pl.BlockSpec((rows, RPB_PAD), lambda i: (0, 0)),
            pl.BlockSpec((1, chunk), lambda i: (0, i)),
        ],
        out_specs=pl.BlockSpec((rows, chunk), lambda i: (0, i)),
        out_shape=jax.ShapeDtypeStruct((rows, n_pos), f32),
        compiler_params=_cparams(1),
    )(rpb, idx)
    return out.reshape(rows, NB_TABLES, NB_Q, NB_BAND)


def _attn_diff_kernel(*refs, tq, cache, lam_init):
    if cache:
        lam_ref, sub_ref, q_ref, k_ref, v_ref, kc_ref, vc_ref, o_ref = refs
    else:
        lam_ref, sub_ref, q_ref, k_ref, v_ref, o_ref = refs
    low = _lane_iota() < HEAD_DIM
    lp = lam_ref[...]
    lam = (jnp.exp(jnp.sum(lp[0:1] * lp[1:2], axis=-1, keepdims=True))
           - jnp.exp(jnp.sum(lp[2:3] * lp[3:4], axis=-1, keepdims=True)) + lam_init)
    for h in range(C_HEADS):
        sl = slice(h * LANES, (h + 1) * LANES)
        q = _pair_lhs(q_ref[:, sl], low)
        parts = [_dot_nt(q, k_ref[:, sl])]
        vs = [v_ref[:, sl]]
        if cache:
            parts.append(_dot_nt(q, kc_ref[:, sl].astype(bf16)))
            vs.append(vc_ref[:, sl].astype(bf16))
        ps, l = _softmax_parts(parts)
        inv = 1.0 / l
        w1 = inv[:tq]
        w2 = lam * inv[tq:]
        o = None
        for p, v in zip(ps, vs):
            a = (p[:tq] * w1 - p[tq:] * w2).astype(bf16)
            t = _dot(a, v)
            o = t if o is None else o + t
        ms = jnp.mean(o * o, axis=-1, keepdims=True)
        o = o * lax.rsqrt(ms + NORM_EPS) * sub_ref[...] * (1.0 - lam_init)
        o_ref[:, sl] = o.astype(o_ref.dtype)


def _attn_diff(lam_p, subln, q, k, v, kc, vc, layer, batch, seq, tq, lam_init):
    cache = kc is not None
    nq = seq // tq
    width = C_HEADS * 2 * HEAD_DIM
    in_specs = [
        pl.BlockSpec((4, LANES), lambda b, j: (0, 0)),
        pl.BlockSpec((1, LANES), lambda b, j: (0, 0)),
        pl.BlockSpec((tq, width), lambda b, j: (b * nq + j, 0)),
        pl.BlockSpec((seq, width), lambda b, j: (b, 0)),
        pl.BlockSpec((seq, width), lambda b, j: (b, 0)),
    ]
    args = [lam_p, subln, q, k, v]
    if cache:
        in_specs += [_cache_spec(layer, width)] * 2
        args += [kc, vc]
    return pl.pallas_call(
        functools.partial(_attn_diff_kernel, tq=tq, cache=cache, lam_init=lam_init),
        grid=(batch, nq),
        in_specs=in_specs,
        out_specs=pl.BlockSpec((tq, width), lambda b, j: (b * nq + j, 0)),
        out_shape=jax.ShapeDtypeStruct((batch * seq, width), bf16),
        compiler_params=_cparams(2),
    )(*args)


def _out_kernel(*refs, n_parts):
    x_ref, mod_ref = refs[:2]
    o_refs = refs[2:2 + n_parts]
    w_refs = refs[2 + n_parts:2 + 2 * n_parts]
    out_ref = refs[2 + 2 * n_parts]
    y = _dot(o_refs[0][...], w_refs[0][...])
    for o_ref, w_ref in zip(o_refs[1:], w_refs[1:]):
        y = y + _dot(o_ref[...], w_ref[...])
    out_ref[...] = x_ref[...] + mod_ref[2:3, :] * y


def _out_proj(x, mod, layer, parts, weights, tm, seq_len):
    n = x.shape[0]
    tiles_per_seq = None if seq_len is None else seq_len // tm
    row = lambda i: (i, 0)
    in_specs = [pl.BlockSpec((tm, D_MODEL), row), _mod_spec(layer, 1, tiles_per_seq)]
    in_specs += [pl.BlockSpec((tm, p.shape[1]), row) for p in parts]
    in_specs += [_const_spec(w.shape) for w in weights]
    return pl.pallas_call(
        functools.partial(_out_kernel, n_parts=len(parts)),
        grid=(n // tm,),
        in_specs=in_specs,
        out_specs=pl.BlockSpec((tm, D_MODEL), row),
        out_shape=jax.ShapeDtypeStruct((n, D_MODEL), f32),
        compiler_params=_cparams(1),
    )(x, mod, *parts, *weights)


def _final_norm_kernel(x_ref, w_ref, o_ref):
    x = x_ref[...]
    ms = jnp.mean(x * x, axis=-1, keepdims=True)
    o_ref[...] = x * lax.rsqrt(ms + NORM_EPS) * w_ref[...]


def _final_norm(x, w, tm):
    n = x.shape[0]
    return pl.pallas_call(
        _final_norm_kernel,
        grid=(n // tm,),
        in_specs=[pl.BlockSpec((tm, D_MODEL), lambda i: (i, 0)), _const_spec((1, D_MODEL))],
        out_specs=pl.BlockSpec((tm, D_MODEL), lambda i: (i, 0)),
        out_shape=jax.ShapeDtypeStruct((n, D_MODEL), f32),
        compiler_params=_cparams(1),
    )(x, w)


def _rope_tables():
    t = jnp.arange(DEC_SEQ)
    row = (t // GRID_W).astype(f32)
    col = (t % GRID_W).astype(f32)
    n_freq = HEAD_DIM // 4
    inv_freq = ROPE_THETA ** (-jnp.arange(n_freq, dtype=f32) / n_freq)
    ang_r = row[:, None] * inv_freq
    ang_c = col[:, None] * inv_freq
    ang = jnp.concatenate([ang_r, ang_r, ang_c, ang_c], axis=-1)
    cos = jnp.cos(ang)
    sin = jnp.sin(ang)
    sign = jnp.asarray(np.tile(np.repeat(np.array([-1.0, 1.0], np.float32), n_freq), 2))
    return jnp.tile(cos, (1, 2)), jnp.tile(sin * sign, (1, 2))


def _chunk_cols(w):
    lead = w.shape[:-2]
    k = w.shape[-2]
    w = w.reshape(lead + (k, N_FF_CHUNKS, FF_CHUNK))
    return jnp.moveaxis(w, -2, -3)


def kernel(x_prompt, x_sample, cache_a_k, cache_a_v, cache_b_k, cache_b_v, cache_c_k, cache_c_v, c, c_ctx,
           w_mod, b_mod, norm_w, ffn_w1, ffn_w3, ffn_w2, w_in_ab, w_out_ab, a_q_norm, a_k_norm, b_rpb,
           w_in_c, w_out_c, c_lambda, c_subln, final_norm):
    lam_inits = [0.8 - 0.6 * math.exp(-0.3 * l) for l in range(DEPTH)]
    n_even = w_in_ab.shape[0]
    n_odd = w_in_c.shape[0]

    mod = _modulation(c, c_ctx, w_mod, b_mod)
    w1 = _chunk_cols(ffn_w1.astype(bf16))
    w3 = _chunk_cols(ffn_w3.astype(bf16))
    w2 = ffn_w2.astype(bf16).reshape(DEPTH, 2, N_FF_CHUNKS, FF_CHUNK, D_MODEL)
    w_in_ab16 = w_in_ab.astype(bf16)
    w_out_ab16 = w_out_ab.astype(bf16)
    w_in_c16 = w_in_c.astype(bf16)
    w_out_c16 = w_out_c.astype(bf16)
    nw = norm_w.reshape(DEPTH, 3, 1, D_MODEL)
    qn = jnp.tile(a_q_norm, (1, 2)).reshape(n_even, 1, LANES)
    kn = jnp.tile(a_k_norm, (1, 2)).reshape(n_even, 1, LANES)
    lam_p = jnp.pad(c_lambda, ((0, 0), (0, 0), (0, LANES - HEAD_DIM)))
    subln = c_subln.reshape(n_odd, 1, 2 * HEAD_DIM)
    gmat = _segment_sum_matrix()
    rope_tabs = _rope_tables()
    nb_bias = _nb_bias_tables(b_rpb)

    ca_k = cache_a_k.reshape(DEC_BATCH, n_even, PAST_LEN, A_KV_HEADS * HEAD_DIM)
    ca_v = cache_a_v.reshape(DEC_BATCH, n_even, PAST_LEN, A_KV_HEADS * HEAD_DIM)
    cb_k = cache_b_k.reshape(DEC_BATCH, n_even, PAST_LEN, B_HEADS * HEAD_DIM)
    cb_v = cache_b_v.reshape(DEC_BATCH, n_even, PAST_LEN, B_HEADS * HEAD_DIM)
    cc_k = cache_c_k.reshape(DEC_BATCH, n_odd, PAST_LEN, C_HEADS * 2 * HEAD_DIM)
    cc_v = cache_c_v.reshape(DEC_BATCH, n_odd, PAST_LEN, C_HEADS * 2 * HEAD_DIM)

    tm = 512
    xs = x_sample.reshape(N_LAT, D_MODEL)
    xp = x_prompt.reshape(N_CTX, D_MODEL)
    new = {name: [] for name in ("ak", "av", "bk", "bv", "ck", "cv")}

    for l in range(DEPTH):
        ffn_args = lambda s: (nw[l, s * 2], w1[l, s], w3[l, s], w2[l, s])
        xs = _ffn(xs, mod, l, 0, *ffn_args(0), tm, DEC_SEQ)
        xp = _ffn(xp, mod, l, 0, *ffn_args(0), tm, None)
        if l % 2 == 0:
            e = l // 2
            pe = lambda x, tabs, ctx: _proj_even(x, mod, l, nw[l, 1], w_in_ab16[e], qn[e], kn[e], gmat,
                                                 tabs, tm, ctx)
            aq, ak, av, bq, bk, bv = pe(xs, rope_tabs, False)
            oa = _attn_a(aq, ak, av, ca_k, ca_v, e, DEC_BATCH, DEC_SEQ, 128)
            ob = _attn_nb(bq, bk, bv, cb_k, cb_v, nb_bias, e)
            w_parts = (w_out_ab16[e, :A_HEADS * HEAD_DIM], w_out_ab16[e, A_HEADS * HEAD_DIM:])
            xs = _out_proj(xs, mod, l, (oa, ob), w_parts, tm, DEC_SEQ)

            aq, ak, av, bq, bk, bv, akf, avf, bkf, bvf = pe(xp, None, True)
            oa = _attn_a(aq, ak, av, None, None, e, BATCH, SEQ, SEQ)
            ob = _attn_pair_ctx(bq, bk, bv, BATCH, SEQ)
            xp = _out_proj(xp, mod, l, (oa, ob), w_parts, tm, None)
            new["ak"].append(akf); new["av"].append(avf); new["bk"].append(bkf); new["bv"].append(bvf)
        else:
            o = l // 2
            po = lambda x, tabs, ctx: _proj_odd(x, mod, l, nw[l, 1], w_in_c16[o], tabs, tm, ctx)
            q, k, v = po(xs, rope_tabs, False)
            oc = _attn_diff(lam_p[o], subln[o], q, k, v, cc_k, cc_v, o, DEC_BATCH, DEC_SEQ, 128,
                            lam_inits[l])
            xs = _out_proj(xs, mod, l, (oc,), (w_out_c16[o],), tm, DEC_SEQ)

            q, k, v, kf, vf = po(xp, None, True)
            oc = _attn_diff(lam_p[o], subln[o], q, k, v, None, None, o, BATCH, SEQ, SEQ, lam_inits[l])
            xp = _out_proj(xp, mod, l, (oc,), (w_out_c16[o],), tm, None)
            new["ck"].append(kf); new["cv"].append(vf)
        xs = _ffn(xs, mod, l, 2, *ffn_args(1), tm, DEC_SEQ)
        xp = _ffn(xp, mod, l, 2, *ffn_args(1), tm, None)

    fn = final_norm.reshape(1, D_MODEL)
    y_sample = _final_norm(xs, fn, tm).reshape(DEC_BATCH, DEC_SEQ, D_MODEL)
    y_prompt = _final_norm(xp, fn, tm).reshape(BATCH, SEQ, D_MODEL)

    def stack(parts, heads, dim):
        return jnp.stack([p.reshape(BATCH, SEQ, heads, dim) for p in parts], axis=1)

    return (y_prompt, y_sample,
            stack(new["ak"], A_KV_HEADS, HEAD_DIM), stack(new["av"], A_KV_HEADS, HEAD_DIM),
            stack(new["bk"], B_HEADS, HEAD_DIM), stack(new["bv"], B_HEADS, HEAD_DIM),
            stack(new["ck"], C_HEADS, 2 * HEAD_DIM), stack(new["cv"], C_HEADS, 2 * HEAD_DIM))
```

```python
import functools
import math

import numpy as np
import jax
import jax.numpy as jnp
from jax import lax
from jax.experimental import pallas as pl
from jax.experimental.pallas import tpu as pltpu

D_MODEL = 1024
BATCH = 16
SEQ = 256
DEPTH = 4
DEC_BATCH = 8
DEC_SEQ = 2048
PAST_LEN = 512
GRID_W = 64
GRID_ROWS = DEC_SEQ // GRID_W
HEAD_DIM = 64
A_HEADS = 8
A_KV_HEADS = 2
B_HEADS = 8
C_HEADS = 8
NA_ROWS = 8
NA_COLS = 16
D_FF = 2816
N_MOD = 9
ROPE_THETA = 10000.0
NORM_EPS = 1e-6
MASK_VALUE = -1e30
AB_IN = (A_HEADS + 2 * A_KV_HEADS + 3 * B_HEADS) * HEAD_DIM
C_IN = 3 * C_HEADS * 2 * HEAD_DIM

LANES = 128
FF_CHUNK = 256
KEY_CHUNK = 512
MOD_ROWS = 16
CTX_MOD_ROW = DEC_BATCH
VMEM_LIMIT = 52 * 1024 * 1024

N_LAT = DEC_BATCH * DEC_SEQ
N_CTX = BATCH * SEQ

LOG2E = math.log2(math.e)
Q_SCALE = HEAD_DIM ** -0.5 * LOG2E

NB_Q_ROWS = 2
NB_Q = NB_Q_ROWS * GRID_W
NB_BAND_ROWS = 10
NB_BAND = NB_BAND_ROWS * GRID_W
NB_BLOCKS = DEC_SEQ // NB_Q
NB_TABLES = 5
RPB_SIZE = (2 * NA_ROWS - 1) * (2 * NA_COLS - 1)
RPB_PAD = 512

bf16 = jnp.bfloat16
f32 = jnp.float32


def _dot(a, b):
    return jnp.dot(a, b, preferred_element_type=f32)


def _dot_nt(a, b):
    return lax.dot_general(a, b, (((1,), (1,)), ((), ())), preferred_element_type=f32)


def _cparams(n_axes):
    return pltpu.CompilerParams(dimension_semantics=("arbitrary",) * n_axes,
                                vmem_limit_bytes=VMEM_LIMIT)


def _const_spec(shape):
    nd = len(shape)
    return pl.BlockSpec(shape, lambda *_: (0,) * nd, pipeline_mode=pl.Buffered(1))


def _modulated_norm(x, mod_ref, nw_ref):
    ms = jnp.mean(x * x, axis=-1, keepdims=True)
    y = x * lax.rsqrt(ms + NORM_EPS) * nw_ref[...]
    return y * (1.0 + mod_ref[1:2, :]) + mod_ref[0:1, :]


def _mod_kernel(c_ref, w_ref, b_ref, o_ref):
    c = c_ref[...]
    s = (c * jax.nn.sigmoid(c)).astype(bf16)
    o_ref[...] = _dot(s, w_ref[...].astype(bf16)) + b_ref[...]


def _modulation(c, c_ctx, w_mod, b_mod):
    cc = jnp.zeros((MOD_ROWS, D_MODEL), f32)
    cc = cc.at[:DEC_BATCH].set(c).at[CTX_MOD_ROW].set(c_ctx)
    out = pl.pallas_call(
        _mod_kernel,
        grid=(DEPTH, N_MOD),
        in_specs=[
            pl.BlockSpec((MOD_ROWS, D_MODEL), lambda l, k: (0, 0)),
            pl.BlockSpec((None, D_MODEL, D_MODEL), lambda l, k: (l, 0, k)),
            pl.BlockSpec((None, None, 1, D_MODEL), lambda l, k: (l, k, 0, 0)),
        ],
        out_specs=pl.BlockSpec((None, None, MOD_ROWS, D_MODEL), lambda l, k: (l, k, 0, 0)),
        out_shape=jax.ShapeDtypeStruct((DEPTH, N_MOD, MOD_ROWS, D_MODEL), f32),
        compiler_params=_cparams(2),
    )(cc, w_mod, b_mod.reshape(DEPTH, N_MOD, 1, D_MODEL))
    return out.reshape(DEPTH, 3, 3, MOD_ROWS, D_MODEL).transpose(0, 1, 3, 2, 4)


def _mod_spec(layer, sub, rows_per_seq):
    if rows_per_seq is None:
        return pl.BlockSpec((None, None, None, 3, D_MODEL),
                            lambda i, *_: (layer, sub, CTX_MOD_ROW, 0, 0))
    return pl.BlockSpec((None, None, None, 3, D_MODEL),
                        lambda i, *_: (layer, sub, i // rows_per_seq, 0, 0))


def _ffn_kernel(x_ref, mod_ref, nw_ref, w1_ref, w3_ref, w2_ref, o_ref):
    x = x_ref[...]
    h = _modulated_norm(x, mod_ref, nw_ref).astype(bf16)
    acc = None
    for c0 in range(0, D_FF, FF_CHUNK):
        a1 = _dot(h, w1_ref[:, c0:c0 + FF_CHUNK])
        a3 = _dot(h, w3_ref[:, c0:c0 + FF_CHUNK])
        g = (a1 * jax.nn.sigmoid(a1) * a3).astype(bf16)
        t = _dot(g, w2_ref[c0:c0 + FF_CHUNK, :])
        acc = t if acc is None else acc + t
    o_ref[...] = x + 0.5 * mod_ref[2:3, :] * acc


def _ffn(x, mod, layer, sub, nw, w1, w3, w2, tm, seq_len):
    n = x.shape[0]
    tiles_per_seq = None if seq_len is None else seq_len // tm
    return pl.pallas_call(
        _ffn_kernel,
        grid=(n // tm,),
        in_specs=[
            pl.BlockSpec((tm, D_MODEL), lambda i: (i, 0)),
            _mod_spec(layer, sub, tiles_per_seq),
            _const_spec((1, D_MODEL)),
            _const_spec((D_MODEL, D_FF)),
            _const_spec((D_MODEL, D_FF)),
            _const_spec((D_FF, D_MODEL)),
        ],
        out_specs=pl.BlockSpec((tm, D_MODEL), lambda i: (i, 0)),
        out_shape=jax.ShapeDtypeStruct((n, D_MODEL), f32),
        compiler_params=_cparams(1),
    )(x, mod, nw, w1, w3, w2)


def _lane_iota():
    return lax.broadcasted_iota(jnp.int32, (1, LANES), 1)


def _head_rms(z, g_ref):
    sq = z * z
    hi = sq.astype(bf16)
    lo = (sq - hi.astype(f32)).astype(bf16)
    return _dot(jnp.concatenate([hi, lo], axis=1), g_ref[...])


def _rope(z, cos, sin_signed, first_half):
    rot = jnp.where(first_half, pltpu.roll(z, LANES - 16, 1), pltpu.roll(z, 16, 1))
    return z * cos + rot * sin_signed


def _segment_sum_matrix():
    blk = np.kron(np.eye(2, dtype=np.float32), np.ones((HEAD_DIM, HEAD_DIM), np.float32)) / HEAD_DIM
    return jnp.asarray(np.concatenate([blk, blk], axis=0), dtype=bf16)


def _proj_even_kernel(*refs, rope, ctx):
    x_ref, mod_ref, nw_ref, w_ref, qn_ref, kn_ref, g_ref = refs[:7]
    pos = 7
    if rope:
        cos_ref, sin_ref = refs[7:9]
        pos = 9
    aq_ref, ak_ref, av_ref, bq_ref, bk_ref, bv_ref = refs[pos:pos + 6]
    if ctx:
        akf_ref, avf_ref, bkf_ref, bvf_ref = refs[pos + 6:pos + 10]

    h = _modulated_norm(x_ref[...], mod_ref, nw_ref).astype(bf16)
    y = _dot(h, w_ref[...])
    lane = _lane_iota()
    low = lane < HEAD_DIM
    first_half = (lane % 32) < 16
    if rope:
        cos = cos_ref[...]
        sin = sin_ref[...]

    for j in range(A_HEADS // 2):
        z = y[:, j * LANES:(j + 1) * LANES]
        z = z * lax.rsqrt(_head_rms(z, g_ref) + NORM_EPS) * qn_ref[...]
        if rope:
            z = _rope(z, cos, sin, first_half)
        z = z * Q_SCALE
        zr = pltpu.roll(z, HEAD_DIM, 1)
        g = j // 2
        keep = low if g == 0 else jnp.logical_not(low)
        h0 = z if g == 0 else zr
        h1 = zr if g == 0 else z
        aq_ref[:, (2 * j) * LANES:(2 * j + 1) * LANES] = jnp.where(keep, h0, 0.0).astype(bf16)
        aq_ref[:, (2 * j + 1) * LANES:(2 * j + 2) * LANES] = jnp.where(keep, h1, 0.0).astype(bf16)

    o = A_HEADS * HEAD_DIM
    z = y[:, o:o + LANES]
    z = z * lax.rsqrt(_head_rms(z, g_ref) + NORM_EPS) * kn_ref[...]
    if ctx:
        akf_ref[...] = z
    if rope:
        z = _rope(z, cos, sin, first_half)
    ak_ref[...] = z.astype(bf16)
    o += LANES
    z = y[:, o:o + LANES]
    if ctx:
        avf_ref[...] = z
    av_ref[...] = z.astype(bf16)
    o += LANES
    nb = B_HEADS * HEAD_DIM
    bq_ref[...] = (y[:, o:o + nb] * Q_SCALE).astype(bf16)
    z = y[:, o + nb:o + 2 * nb]
    if ctx:
        bkf_ref[...] = z
    bk_ref[...] = z.astype(bf16)
    z = y[:, o + 2 * nb:o + 3 * nb]
    if ctx:
        bvf_ref[...] = z
    bv_ref[...] = z.astype(bf16)


def _proj_even(x, mod, layer, nw, w_in, qn, kn, gmat, rope_tabs, tm, ctx):
    n = x.shape[0]
    rope = rope_tabs is not None
    tiles_per_seq = None if ctx else DEC_SEQ // tm
    row = lambda i: (i, 0)
    in_specs = [
        pl.BlockSpec((tm, D_MODEL), row),
        _mod_spec(layer, 1, tiles_per_seq),
        _const_spec((1, D_MODEL)),
        _const_spec((D_MODEL, AB_IN)),
        _const_spec((1, LANES)),
        _const_spec((1, LANES)),
        _const_spec((2 * LANES, LANES)),
    ]
    args = [x, mod, nw, w_in, qn, kn, gmat]
    if rope:
        pos_spec = pl.BlockSpec((tm, LANES), lambda i: (i % tiles_per_seq, 0))
        in_specs += [pos_spec, pos_spec]
        args += list(rope_tabs)
    widths = [2 * A_HEADS * HEAD_DIM, LANES, LANES, 512, 512, 512]
    out_shape = [jax.ShapeDtypeStruct((n, w), bf16) for w in widths]
    out_specs = [pl.BlockSpec((tm, w), row) for w in widths]
    if ctx:
        for w in (LANES, LANES, 512, 512):
            out_shape.append(jax.ShapeDtypeStruct((n, w), f32))
            out_specs.append(pl.BlockSpec((tm, w), row))
    return pl.pallas_call(
        functools.partial(_proj_even_kernel, rope=rope, ctx=ctx),
        grid=(n // tm,),
        in_specs=in_specs,
        out_specs=out_specs,
        out_shape=out_shape,
        compiler_params=_cparams(1),
    )(*args)


def _proj_odd_kernel(*refs, rope, ctx):
    x_ref, mod_ref, nw_ref, w_ref = refs[:4]
    pos = 4
    if rope:
        cos_ref, sin_ref = refs[4:6]
        pos = 6
    q_ref, k_ref, v_ref = refs[pos:pos + 3]
    if ctx:
        kf_ref, vf_ref = refs[pos + 3:pos + 5]

    h = _modulated_norm(x_ref[...], mod_ref, nw_ref).astype(bf16)
    y = _dot(h, w_ref[...])
    lane = _lane_iota()
    first_half = (lane % 32) < 16
    width = C_HEADS * 2 * HEAD_DIM
    for j in range(width // LANES):
        zq = y[:, j * LANES:(j + 1) * LANES]
        zk = y[:, width + j * LANES:width + (j + 1) * LANES]
        if ctx:
            kf_ref[:, j * LANES:(j + 1) * LANES] = zk
        if rope:
            zq = _rope(zq, cos_ref[...], sin_ref[...], first_half)
            zk = _rope(zk, cos_ref[...], sin_ref[...], first_half)
        q_ref[:, j * LANES:(j + 1) * LANES] = (zq * Q_SCALE).astype(bf16)
        k_ref[:, j * LANES:(j + 1) * LANES] = zk.astype(bf16)
    zv = y[:, 2 * width:3 * width]
    if ctx:
        vf_ref[...] = zv
    v_ref[...] = zv.astype(bf16)


def _proj_odd(x, mod, layer, nw, w_in, rope_tabs, tm, ctx):
    n = x.shape[0]
    rope = rope_tabs is not None
    tiles_per_seq = None if ctx else DEC_SEQ // tm
    row = lambda i: (i, 0)
    width = C_HEADS * 2 * HEAD_DIM
    in_specs = [
        pl.BlockSpec((tm, D_MODEL), row),
        _mod_spec(layer, 1, tiles_per_seq),
        _const_spec((1, D_MODEL)),
        _const_spec((D_MODEL, C_IN)),
    ]
    args = [x, mod, nw, w_in]
    if rope:
        pos_spec = pl.BlockSpec((tm, LANES), lambda i: (i % tiles_per_seq, 0))
        in_specs += [pos_spec, pos_spec]
        args += list(rope_tabs)
    out_shape = [jax.ShapeDtypeStruct((n, width), bf16)] * 3
    out_specs = [pl.BlockSpec((tm, width), row)] * 3
    if ctx:
        out_shape = out_shape + [jax.ShapeDtypeStruct((n, width), f32)] * 2
        out_specs = out_specs + [pl.BlockSpec((tm, width), row)] * 2
    return pl.pallas_call(
        functools.partial(_proj_odd_kernel, rope=rope, ctx=ctx),
        grid=(n // tm,),
        in_specs=in_specs,
        out_specs=out_specs,
        out_shape=out_shape,
        compiler_params=_cparams(1),
    )(*args)


def _lane_tile_reduce(x, op):
    acc = x[:, :LANES]
    for i in range(1, x.shape[1] // LANES):
        acc = op(acc, x[:, i * LANES:(i + 1) * LANES])
    return acc


def _store_scores(s_ref, q, keys, bias=None):
    off = 0
    for i, k in enumerate(keys):
        s = _dot_nt(q, k)
        if i == 0 and bias is not None:
            s = s + bias
        s_ref[:, off:off + k.shape[0]] = s
        off += k.shape[0]
    return _lane_tile_reduce(s_ref[...], jnp.maximum).max(axis=-1, keepdims=True)


def _softmax_values(s_ref, m, values):
    l_acc = None
    o = None
    off = 0
    for load, n in values:
        for c0 in range(0, n, KEY_CHUNK):
            c1 = min(c0 + KEY_CHUNK, n)
            p = jnp.exp2(s_ref[:, off + c0:off + c1] - m)
            lt = _lane_tile_reduce(p, jnp.add)
            l_acc = lt if l_acc is None else l_acc + lt
            t = _dot(p.astype(bf16), load(c0, c1))
            o = t if o is None else o + t
        off += n
    return o, l_acc.sum(axis=-1, keepdims=True)


def _run_units(n_units, s_refs, scores, finish):
    m_next = scores(0, s_refs[0])
    for u in range(n_units):
        m = m_next
        if u + 1 < n_units:
            m_next = scores(u + 1, s_refs[(u + 1) % 2])
        finish(u, s_refs[u % 2], m)


def _score_scratch(rows, n_keys):
    return [pltpu.VMEM((rows, n_keys), f32)] * 2


def _cache_spec(layer, width, col_block=lambda *idx: 0):
    return pl.BlockSpec((None, None, PAST_LEN, width), lambda b, *rest: (b, layer, 0, col_block(b, *rest)))


def _attn_a_kernel(*refs, tq, cache):
    if cache:
        q_ref, k_ref, v_ref, kc_ref, vc_ref, o_ref, s0_ref, s1_ref = refs
    else:
        q_ref, k_ref, v_ref, o_ref, s0_ref, s1_ref = refs
    low = _lane_iota() < HEAD_DIM
    group = A_HEADS // A_KV_HEADS
    seq = k_ref.shape[0]

    def scores(g, s_ref):
        q = jnp.concatenate(
            [q_ref[:, (group * g + i) * LANES:(group * g + i + 1) * LANES] for i in range(group)], axis=0)
        keys = [k_ref[...]]
        if cache:
            keys.append(kc_ref[...].astype(bf16))
        return _store_scores(s_ref, q, keys)

    def finish(g, s_ref, m):
        values = [(lambda a, b: v_ref[a:b, :], seq)]
        if cache:
            values.append((lambda a, b: vc_ref[a:b, :].astype(bf16), PAST_LEN))
        o, l = _softmax_values(s_ref, m, values)
        o = o / l
        o_other = pltpu.roll(o, HEAD_DIM, 1)
        o_low, o_high = (o, o_other) if g == 0 else (o_other, o)
        for jj in range(group // 2):
            even = o_low[(2 * jj) * tq:(2 * jj + 1) * tq]
            odd = o_high[(2 * jj + 1) * tq:(2 * jj + 2) * tq]
            blk = (group // 2) * g + jj
            o_ref[:, blk * LANES:(blk + 1) * LANES] = jnp.where(low, even, odd).astype(o_ref.dtype)

    _run_units(A_KV_HEADS, (s0_ref, s1_ref), scores, finish)


def _attn_a(q, k, v, kc, vc, layer, batch, seq, tq):
    cache = kc is not None
    nq = seq // tq
    in_specs = [
        pl.BlockSpec((tq, 2 * A_HEADS * HEAD_DIM), lambda b, j: (b * nq + j, 0)),
        pl.BlockSpec((seq, LANES), lambda b, j: (b, 0)),
        pl.BlockSpec((seq, LANES), lambda b, j: (b, 0)),
    ]
    args = [q, k, v]
    if cache:
        in_specs += [_cache_spec(layer, LANES)] * 2
        args += [kc, vc]
    n_keys = seq + (PAST_LEN if cache else 0)
    return pl.pallas_call(
        functools.partial(_attn_a_kernel, tq=tq, cache=cache),
        grid=(batch, nq),
        in_specs=in_specs,
        out_specs=pl.BlockSpec((tq, A_HEADS * HEAD_DIM), lambda b, j: (b * nq + j, 0)),
        out_shape=jax.ShapeDtypeStruct((batch * seq, A_HEADS * HEAD_DIM), bf16),
        scratch_shapes=_score_scratch(A_HEADS // A_KV_HEADS * tq, n_keys),
        compiler_params=_cparams(2),
    )(*args)


def _pair_lhs(q, low):
    zero = jnp.zeros_like(q)
    return jnp.concatenate([jnp.where(low, q, zero), jnp.where(low, zero, q)], axis=0)


def _attn_pair_ctx_kernel(q_ref, k_ref, v_ref, o_ref, s0_ref, s1_ref, *, tq):
    low = _lane_iota() < HEAD_DIM
    seq = k_ref.shape[0]

    def scores(hp, s_ref):
        sl = slice(hp * LANES, (hp + 1) * LANES)
        return _store_scores(s_ref, _pair_lhs(q_ref[:, sl], low), [k_ref[:, sl]])

    def finish(hp, s_ref, m):
        sl = slice(hp * LANES, (hp + 1) * LANES)
        o, l = _softmax_values(s_ref, m, [(lambda a, b: v_ref[a:b, sl], seq)])
        o = o / l
        o_ref[:, sl] = jnp.where(low, o[:tq], o[tq:]).astype(o_ref.dtype)

    _run_units(B_HEADS // 2, (s0_ref, s1_ref), scores, finish)


def _attn_pair_ctx(q, k, v, batch, seq):
    spec = pl.BlockSpec((seq, B_HEADS * HEAD_DIM), lambda b: (b, 0))
    return pl.pallas_call(
        functools.partial(_attn_pair_ctx_kernel, tq=seq),
        grid=(batch,),
        in_specs=[spec, spec, spec],
        out_specs=spec,
        out_shape=jax.ShapeDtypeStruct((batch * seq, B_HEADS * HEAD_DIM), bf16),
        scratch_shapes=_score_scratch(2 * seq, seq),
        compiler_params=_cparams(1),
    )(q, k, v)


def _nb_band_start(j):
    return jnp.clip(NB_Q_ROWS * j - NA_ROWS // 2, 0, GRID_ROWS - NB_BAND_ROWS)


def _nb_table_index(j):
    return jnp.where(j < 2, j, jnp.where(j >= NB_BLOCKS - 2, j - (NB_BLOCKS - NB_TABLES), 2))


def _attn_nb_kernel(q_ref, k_ref, v_ref, kc_ref, vc_ref, bias_ref, o_ref, s0_ref, s1_ref):
    j = pl.program_id(1)
    start = pl.multiple_of(_nb_band_start(j) * GRID_W, GRID_W)
    low = _lane_iota() < HEAD_DIM

    def scores(hp, s_ref):
        sl = slice(hp * LANES, (hp + 1) * LANES)
        q = _pair_lhs(q_ref[:, sl], low)
        bias = jnp.concatenate([bias_ref[2 * hp], bias_ref[2 * hp + 1]], axis=0)
        keys = [k_ref[pl.ds(start, NB_BAND), sl], kc_ref[:, sl].astype(bf16)]
        return _store_scores(s_ref, q, keys, bias)

    def finish(hp, s_ref, m):
        sl = slice(hp * LANES, (hp + 1) * LANES)
        values = [(lambda a, b: v_ref[pl.ds(start + a, b - a), sl], NB_BAND),
                  (lambda a, b: vc_ref[a:b, sl].astype(bf16), PAST_LEN)]
        o, l = _softmax_values(s_ref, m, values)
        o = o / l
        o_ref[:, sl] = jnp.where(low, o[:NB_Q], o[NB_Q:]).astype(o_ref.dtype)

    _run_units(B_HEADS // 2, (s0_ref, s1_ref), scores, finish)


def _attn_nb(q, k, v, kc, vc, bias, layer):
    width = B_HEADS * HEAD_DIM
    return pl.pallas_call(
        _attn_nb_kernel,
        grid=(DEC_BATCH, NB_BLOCKS),
        in_specs=[
            pl.BlockSpec((NB_Q, width), lambda b, j: (b * NB_BLOCKS + j, 0)),
            pl.BlockSpec((DEC_SEQ, width), lambda b, j: (b, 0)),
            pl.BlockSpec((DEC_SEQ, width), lambda b, j: (b, 0)),
            _cache_spec(layer, width),
            _cache_spec(layer, width),
            pl.BlockSpec((B_HEADS, None, NB_Q, NB_BAND), lambda b, j: (layer, _nb_table_index(j), 0, 0)),
        ],
        out_specs=pl.BlockSpec((NB_Q, width), lambda b, j: (b * NB_BLOCKS + j, 0)),
        out_shape=jax.ShapeDtypeStruct((N_LAT, width), bf16),
        scratch_shapes=_score_scratch(2 * NB_Q, NB_BAND + PAST_LEN),
        compiler_params=_cparams(2),
    )(q, k, v, kc, vc, bias)


def _nb_bias_index():
    idx = np.full((NB_TABLES, NB_Q, NB_BAND), -1, np.int32)
    blocks = [0, 1, 2, NB_BLOCKS - 2, NB_BLOCKS - 1]
    win_c = min(NA_COLS, GRID_W)
    for t, j in enumerate(blocks):
        band_start = int(np.clip(NB_Q_ROWS * j - NA_ROWS // 2, 0, GRID_ROWS - NB_BAND_ROWS))
        ql = np.arange(NB_Q)
        q_row = j * NB_Q_ROWS + ql // GRID_W
        q_col = ql % GRID_W
        kl = np.arange(NB_BAND)
        k_row = band_start + kl // GRID_W
        k_col = kl % GRID_W
        row_start = np.clip(q_row - NA_ROWS // 2, 0, GRID_ROWS - NA_ROWS)
        col_start = np.clip(q_col - win_c // 2, 0, GRID_W - win_c)
        row_in = (k_row[None] >= row_start[:, None]) & (k_row[None] < row_start[:, None] + NA_ROWS)
        col_in = (k_col[None] >= col_start[:, None]) & (k_col[None] < col_start[:, None] + win_c)
        dr = np.clip(k_row[None] - q_row[:, None] + NA_ROWS - 1, 0, 2 * NA_ROWS - 2)
        dc = np.clip(k_col[None] - q_col[:, None] + NA_COLS - 1, 0, 2 * NA_COLS - 2)
        flat = dr * (2 * NA_COLS - 1) + dc
        idx[t] = np.where(row_in & col_in, flat, -1)
    return idx.reshape(1, -1)


def _nb_bias_kernel(rpb_ref, idx_ref, o_ref):
    r = rpb_ref[...]
    hi = r.astype(bf16)
    r1 = r - hi.astype(f32)
    mid = r1.astype(bf16)
    lo = (r1 - mid.astype(f32)).astype(bf16)
    idx = idx_ref[...]
    onehot = (lax.broadcasted_iota(jnp.int32, (RPB_PAD, idx.shape[1]), 0) == idx)
    onehot = jnp.where(onehot, 1.0, 0.0).astype(bf16)
    val = (_dot(lo, onehot) + _dot(mid, onehot)) + _dot(hi, onehot)
    o_ref[...] = jnp.where(idx >= 0, val * LOG2E, MASK_VALUE)


def _nb_bias_tables(b_rpb):
    n_even = b_rpb.shape[0]
    rows = n_even * B_HEADS
    rpb = jnp.pad(b_rpb.reshape(rows, RPB_SIZE), ((0, 0), (0, RPB_PAD - RPB_SIZE)))
    idx = jnp.asarray(_nb_bias_index())
    n_pos = idx.shape[1]
    chunk = 2048
    out = pl.pallas_call(
        _nb_bias_kernel,
        grid=(n_pos // chunk,),
        in_specs=[
            pl.BlockSpec((rows, RPB_PAD), lambda i: (0, 0)),
            pl.BlockSpec((1, chunk), lambda i: (0, i)),
        ],
        out_specs=pl.BlockSpec((rows, chunk), lambda i: (0, i)),
        out_shape=jax.ShapeDtypeStruct((rows, n_pos), f32),
        compiler_params=_cparams(1),
    )(rpb, idx)
    return out.reshape(rows, NB_TABLES, NB_Q, NB_BAND)


def _attn_diff_kernel(*refs, tq, cache, lam_init, n_heads):
    if cache:
        lam_ref, sub_ref, q_ref, k_ref, v_ref, kc_ref, vc_ref, o_ref, s0_ref, s1_ref = refs
    else:
        lam_ref, sub_ref, q_ref, k_ref, v_ref, o_ref, s0_ref, s1_ref = refs
    low = _lane_iota() < HEAD_DIM
    seq = k_ref.shape[0]
    lp = lam_ref[...]
    lam = (jnp.exp(jnp.sum(lp[0:1] * lp[1:2], axis=-1, keepdims=True))
           - jnp.exp(jnp.sum(lp[2:3] * lp[3:4], axis=-1, keepdims=True)) + lam_init)

    def scores(h, s_ref):
        sl = slice(h * LANES, (h + 1) * LANES)
        keys = [k_ref[:, sl]]
        if cache:
            keys.append(kc_ref[:, sl].astype(bf16))
        return _store_scores(s_ref, _pair_lhs(q_ref[:, sl], low), keys)

    def finish(h, s_ref, m):
        sl = slice(h * LANES, (h + 1) * LANES)
        values = [(lambda a, b: v_ref[a:b, sl], seq)]
        if cache:
            values.append((lambda a, b: vc_ref[a:b, sl].astype(bf16), PAST_LEN))
        o, l = _softmax_values(s_ref, m, values)
        inv = 1.0 / l
        o = o[:tq] * inv[:tq] - o[tq:] * (lam * inv[tq:])
        ms = jnp.mean(o * o, axis=-1, keepdims=True)
        o = o * lax.rsqrt(ms + NORM_EPS) * sub_ref[...] * (1.0 - lam_init)
        o_ref[:, sl] = o.astype(o_ref.dtype)

    _run_units(n_heads, (s0_ref, s1_ref), scores, finish)


def _attn_diff(lam_p, subln, q, k, v, kc, vc, layer, batch, seq, tq, heads_per_step, lam_init):
    cache = kc is not None
    nq = seq // tq
    width = C_HEADS * 2 * HEAD_DIM
    gw = heads_per_step * LANES
    in_specs = [
        pl.BlockSpec((4, LANES), lambda b, g, j: (0, 0)),
        pl.BlockSpec((1, LANES), lambda b, g, j: (0, 0)),
        pl.BlockSpec((tq, gw), lambda b, g, j: (b * nq + j, g)),
        pl.BlockSpec((seq, gw), lambda b, g, j: (b, g)),
        pl.BlockSpec((seq, gw), lambda b, g, j: (b, g)),
    ]
    args = [lam_p, subln, q, k, v]
    if cache:
        in_specs += [_cache_spec(layer, gw, lambda b, g, j: g)] * 2
        args += [kc, vc]
    n_keys = seq + (PAST_LEN if cache else 0)
    return pl.pallas_call(
        functools.partial(_attn_diff_kernel, tq=tq, cache=cache, lam_init=lam_init, n_heads=heads_per_step),
        grid=(batch, C_HEADS // heads_per_step, nq),
        in_specs=in_specs,
        out_specs=pl.BlockSpec((tq, gw), lambda b, g, j: (b * nq + j, g)),
        out_shape=jax.ShapeDtypeStruct((batch * seq, width), bf16),
        scratch_shapes=_score_scratch(2 * tq, n_keys),
        compiler_params=_cparams(3),
    )(*args)


def _out_kernel(*refs, n_parts):
    x_ref, mod_ref = refs[:2]
    o_refs = refs[2:2 + n_parts]
    w_refs = refs[2 + n_parts:2 + 2 * n_parts]
    out_ref = refs[2 + 2 * n_parts]
    y = _dot(o_refs[0][...], w_refs[0][...])
    for o_ref, w_ref in zip(o_refs[1:], w_refs[1:]):
        y = y + _dot(o_ref[...], w_ref[...])
    out_ref[...] = x_ref[...] + mod_ref[2:3, :] * y


def _out_proj(x, mod, layer, parts, weights, tm, seq_len):
    n = x.shape[0]
    tiles_per_seq = None if seq_len is None else seq_len // tm
    row = lambda i: (i, 0)
    in_specs = [pl.BlockSpec((tm, D_MODEL), row), _mod_spec(layer, 1, tiles_per_seq)]
    in_specs += [pl.BlockSpec((tm, p.shape[1]), row) for p in parts]
    in_specs += [_const_spec(w.shape) for w in weights]
    return pl.pallas_call(
        functools.partial(_out_kernel, n_parts=len(parts)),
        grid=(n // tm,),
        in_specs=in_specs,
        out_specs=pl.BlockSpec((tm, D_MODEL), row),
        out_shape=jax.ShapeDtypeStruct((n, D_MODEL), f32),
        compiler_params=_cparams(1),
    )(x, mod, *parts, *weights)


def _final_norm_kernel(x_ref, w_ref, o_ref):
    x = x_ref[...]
    ms = jnp.mean(x * x, axis=-1, keepdims=True)
    o_ref[...] = x * lax.rsqrt(ms + NORM_EPS) * w_ref[...]


def _final_norm(x, w, tm):
    n = x.shape[0]
    return pl.pallas_call(
        _final_norm_kernel,
        grid=(n // tm,),
        in_specs=[pl.BlockSpec((tm, D_MODEL), lambda i: (i, 0)), _const_spec((1, D_MODEL))],
        out_specs=pl.BlockSpec((tm, D_MODEL), lambda i: (i, 0)),
        out_shape=jax.ShapeDtypeStruct((n, D_MODEL), f32),
        compiler_params=_cparams(1),
    )(x, w)


def _rope_tables():
    t = jnp.arange(DEC_SEQ)
    row = (t // GRID_W).astype(f32)
    col = (t % GRID_W).astype(f32)
    n_freq = HEAD_DIM // 4
    inv_freq = ROPE_THETA ** (-jnp.arange(n_freq, dtype=f32) / n_freq)
    ang_r = row[:, None] * inv_freq
    ang_c = col[:, None] * inv_freq
    ang = jnp.concatenate([ang_r, ang_r, ang_c, ang_c], axis=-1)
    cos = jnp.cos(ang)
    sin = jnp.sin(ang)
    sign = jnp.asarray(np.tile(np.repeat(np.array([-1.0, 1.0], np.float32), n_freq), 2))
    return jnp.tile(cos, (1, 2)), jnp.tile(sin * sign, (1, 2))


def kernel(x_prompt, x_sample, cache_a_k, cache_a_v, cache_b_k, cache_b_v, cache_c_k, cache_c_v, c, c_ctx,
           w_mod, b_mod, norm_w, ffn_w1, ffn_w3, ffn_w2, w_in_ab, w_out_ab, a_q_norm, a_k_norm, b_rpb,
           w_in_c, w_out_c, c_lambda, c_subln, final_norm):
    lam_inits = [0.8 - 0.6 * math.exp(-0.3 * l) for l in range(DEPTH)]
    n_even = w_in_ab.shape[0]
    n_odd = w_in_c.shape[0]

    mod = _modulation(c, c_ctx, w_mod, b_mod)
    w1 = ffn_w1.astype(bf16)
    w3 = ffn_w3.astype(bf16)
    w2 = ffn_w2.astype(bf16)
    w_in_ab16 = w_in_ab.astype(bf16)
    w_out_ab16 = w_out_ab.astype(bf16)
    w_in_c16 = w_in_c.astype(bf16)
    w_out_c16 = w_out_c.astype(bf16)
    nw = norm_w.reshape(DEPTH, 3, 1, D_MODEL)
    qn = jnp.tile(a_q_norm, (1, 2)).reshape(n_even, 1, LANES)
    kn = jnp.tile(a_k_norm, (1, 2)).reshape(n_even, 1, LANES)
    lam_p = jnp.pad(c_lambda, ((0, 0), (0, 0), (0, LANES - HEAD_DIM)))
    subln = c_subln.reshape(n_odd, 1, 2 * HEAD_DIM)
    gmat = _segment_sum_matrix()
    rope_tabs = _rope_tables()
    nb_bias = _nb_bias_tables(b_rpb)

    ca_k = cache_a_k.reshape(DEC_BATCH, n_even, PAST_LEN, A_KV_HEADS * HEAD_DIM)
    ca_v = cache_a_v.reshape(DEC_BATCH, n_even, PAST_LEN, A_KV_HEADS * HEAD_DIM)
    cb_k = cache_b_k.reshape(DEC_BATCH, n_even, PAST_LEN, B_HEADS * HEAD_DIM)
    cb_v = cache_b_v.reshape(DEC_BATCH, n_even, PAST_LEN, B_HEADS * HEAD_DIM)
    cc_k = cache_c_k.reshape(DEC_BATCH, n_odd, PAST_LEN, C_HEADS * 2 * HEAD_DIM)
    cc_v = cache_c_v.reshape(DEC_BATCH, n_odd, PAST_LEN, C_HEADS * 2 * HEAD_DIM)

    tm = 512
    xs = x_sample.reshape(N_LAT, D_MODEL)
    xp = x_prompt.reshape(N_CTX, D_MODEL)
    new = {name: [] for name in ("ak", "av", "bk", "bv", "ck", "cv")}

    for l in range(DEPTH):
        ffn_args = lambda s: (nw[l, s * 2], w1[l, s], w3[l, s], w2[l, s])
        xs = _ffn(xs, mod, l, 0, *ffn_args(0), tm, DEC_SEQ)
        xp = _ffn(xp, mod, l, 0, *ffn_args(0), tm, None)
        if l % 2 == 0:
            e = l // 2
            pe = lambda x, tabs, ctx: _proj_even(x, mod, l, nw[l, 1], w_in_ab16[e], qn[e], kn[e], gmat,
                                                 tabs, tm, ctx)
            aq, ak, av, bq, bk, bv = pe(xs, rope_tabs, False)
            oa = _attn_a(aq, ak, av, ca_k, ca_v, e, DEC_BATCH, DEC_SEQ, 128)
            ob = _attn_nb(bq, bk, bv, cb_k, cb_v, nb_bias, e)
            w_parts = (w_out_ab16[e, :A_HEADS * HEAD_DIM], w_out_ab16[e, A_HEADS * HEAD_DIM:])
            xs = _out_proj(xs, mod, l, (oa, ob), w_parts, tm, DEC_SEQ)

            aq, ak, av, bq, bk, bv, akf, avf, bkf, bvf = pe(xp, None, True)
            oa = _attn_a(aq, ak, av, None, None, e, BATCH, SEQ, SEQ)
            ob = _attn_pair_ctx(bq, bk, bv, BATCH, SEQ)
            xp = _out_proj(xp, mod, l, (oa, ob), w_parts, tm, None)
            new["ak"].append(akf); new["av"].append(avf); new["bk"].append(bkf); new["bv"].append(bvf)
        else:
            o = l // 2
            po = lambda x, tabs, ctx: _proj_odd(x, mod, l, nw[l, 1], w_in_c16[o], tabs, tm, ctx)
            q, k, v = po(xs, rope_tabs, False)
            oc = _attn_diff(lam_p[o], subln[o], q, k, v, cc_k, cc_v, o, DEC_BATCH, DEC_SEQ, 256, 4,
                            lam_inits[l])
            xs = _out_proj(xs, mod, l, (oc,), (w_out_c16[o],), tm, DEC_SEQ)

            q, k, v, kf, vf = po(xp, None, True)
            oc = _attn_diff(lam_p[o], subln[o], q, k, v, None, None, o, BATCH, SEQ, SEQ, C_HEADS,
                            lam_inits[l])
            xp = _out_proj(xp, mod, l, (oc,), (w_out_c16[o],), tm, None)
            new["ck"].append(kf); new["cv"].append(vf)
        xs = _ffn(xs, mod, l, 2, *ffn_args(1), tm, DEC_SEQ)
        xp = _ffn(xp, mod, l, 2, *ffn_args(1), tm, None)

    fn = final_norm.reshape(1, D_MODEL)
    y_sample = _final_norm(xs, fn, tm).reshape(DEC_BATCH, DEC_SEQ, D_MODEL)
    y_prompt = _final_norm(xp, fn, tm).reshape(BATCH, SEQ, D_MODEL)

    def stack(parts, heads, dim):
        return jnp.stack([p.reshape(BATCH, SEQ, heads, dim) for p in parts], axis=1)

    return (y_prompt, y_sample,
            stack(new["ak"], A_KV_HEADS, HEAD_DIM), stack(new["av"], A_KV_HEADS, HEAD_DIM),
            stack(new["bk"], B_HEADS, HEAD_DIM), stack(new["bv"], B_HEADS, HEAD_DIM),
            stack(new["ck"], C_HEADS, 2 * HEAD_DIM), stack(new["cv"], C_HEADS, 2 * HEAD_DIM))
```

```python
import functools
import math

import numpy as np
import jax
import jax.numpy as jnp
from jax import lax
from jax.experimental import pallas as pl
from jax.experimental.pallas import tpu as pltpu

D_MODEL = 1024
BATCH = 16
SEQ = 256
DEPTH = 4
DEC_BATCH = 8
DEC_SEQ = 2048
PAST_LEN = 512
GRID_W = 64
GRID_ROWS = DEC_SEQ // GRID_W
HEAD_DIM = 64
A_HEADS = 8
A_KV_HEADS = 2
B_HEADS = 8
C_HEADS = 8
NA_ROWS = 8
NA_COLS = 16
D_FF = 2816
N_MOD = 9
ROPE_THETA = 10000.0
NORM_EPS = 1e-6
MASK_VALUE = -1e30
AB_IN = (A_HEADS + 2 * A_KV_HEADS + 3 * B_HEADS) * HEAD_DIM
AB_OUT = (A_HEADS + B_HEADS) * HEAD_DIM
C_IN = 3 * C_HEADS * 2 * HEAD_DIM
C_OUT = C_HEADS * 2 * HEAD_DIM

LANES = 128
FF_CHUNK = 256
KEY_CHUNK = 512
MOD_ROWS = 16
CTX_MOD_ROW = DEC_BATCH
VMEM_LIMIT = 52 * 1024 * 1024
TOKEN_TILE = 512

N_LAT = DEC_BATCH * DEC_SEQ
N_CTX = BATCH * SEQ

LOG2E = math.log2(math.e)
Q_SCALE = HEAD_DIM ** -0.5 * LOG2E

A_UNIT_Q = 128

NB_Q_ROWS = 2
NB_Q = NB_Q_ROWS * GRID_W
NB_BAND_ROWS = 10
NB_BAND = NB_BAND_ROWS * GRID_W
NB_BLOCKS = DEC_SEQ // NB_Q
NB_STEP_BLOCKS = 2
NB_TABLE_BLOCKS = (0, 1, 2, NB_BLOCKS - 2, NB_BLOCKS - 1)
NB_TABLES = len(NB_TABLE_BLOCKS)
RPB_ROWS = 2 * NA_ROWS - 1
RPB_COLS = 2 * NA_COLS - 1

bf16 = jnp.bfloat16
f32 = jnp.float32


def _dot(a, b):
    return jnp.dot(a, b, preferred_element_type=f32)


def _dot_nt(a, b):
    return lax.dot_general(a, b, (((1,), (1,)), ((), ())), preferred_element_type=f32)


def _cparams(n_axes):
    return pltpu.CompilerParams(dimension_semantics=("arbitrary",) * n_axes,
                                vmem_limit_bytes=VMEM_LIMIT)


def _resident_spec(shape, *lead):
    block = (None,) * len(lead) + tuple(shape)
    index = tuple(lead) + (0,) * len(shape)
    return pl.BlockSpec(block, lambda *_: index, pipeline_mode=pl.Buffered(1))


def _modulated_norm(x, mod_ref, nw_ref):
    ms = jnp.mean(x * x, axis=-1, keepdims=True)
    y = x * lax.rsqrt(ms + NORM_EPS) * nw_ref[...]
    return y * (1.0 + mod_ref[1:2, :]) + mod_ref[0:1, :]


def _mod_kernel(c_ref, w_ref, b_ref, o_ref):
    c = c_ref[...]
    s = (c * jax.nn.sigmoid(c)).astype(bf16)
    o_ref[...] = _dot(s, w_ref[...].astype(bf16)) + b_ref[...]


def _modulation(c, c_ctx, w_mod, b_mod):
    cc = jnp.zeros((MOD_ROWS, D_MODEL), f32)
    cc = cc.at[:DEC_BATCH].set(c).at[CTX_MOD_ROW].set(c_ctx)
    out = pl.pallas_call(
        _mod_kernel,
        grid=(DEPTH, N_MOD),
        in_specs=[
            pl.BlockSpec((MOD_ROWS, D_MODEL), lambda l, k: (0, 0)),
            pl.BlockSpec((None, D_MODEL, D_MODEL), lambda l, k: (l, 0, k)),
            pl.BlockSpec((None, None, 1, D_MODEL), lambda l, k: (l, k, 0, 0)),
        ],
        out_specs=pl.BlockSpec((None, None, MOD_ROWS, D_MODEL), lambda l, k: (l, k, 0, 0)),
        out_shape=jax.ShapeDtypeStruct((DEPTH, N_MOD, MOD_ROWS, D_MODEL), f32),
        compiler_params=_cparams(2),
    )(cc, w_mod, b_mod.reshape(DEPTH, N_MOD, 1, D_MODEL))
    return out.reshape(DEPTH, 3, 3, MOD_ROWS, D_MODEL).transpose(0, 1, 3, 2, 4)


def _mod_spec(layer, sub, tiles_per_seq):
    if tiles_per_seq is None:
        return pl.BlockSpec((None, None, None, 3, D_MODEL),
                            lambda i, *_: (layer, sub, CTX_MOD_ROW, 0, 0))
    return pl.BlockSpec((None, None, None, 3, D_MODEL),
                        lambda i, *_: (layer, sub, i // tiles_per_seq, 0, 0))


def _ffn_kernel(*refs, n_parts):
    x_ref = refs[0]
    x = x_ref[...]
    pos = 1
    if n_parts:
        mix_mod_ref = refs[1]
        o_refs = refs[2:2 + n_parts]
        wo_refs = refs[2 + n_parts:2 + 2 * n_parts]
        pos = 2 + 2 * n_parts
        y = _dot(o_refs[0][...], wo_refs[0][...])
        for o_ref, w_ref in zip(o_refs[1:], wo_refs[1:]):
            y = y + _dot(o_ref[...], w_ref[...])
        x = x + mix_mod_ref[2:3, :] * y
    mod_ref, nw_ref, w1_ref, w3_ref, w2_ref, out_ref = refs[pos:]
    h = _modulated_norm(x, mod_ref, nw_ref).astype(bf16)
    acc = None
    for c0 in range(0, D_FF, FF_CHUNK):
        a1 = _dot(h, w1_ref[:, c0:c0 + FF_CHUNK])
        a3 = _dot(h, w3_ref[:, c0:c0 + FF_CHUNK])
        g = (a1 * jax.nn.sigmoid(a1) * a3).astype(bf16)
        t = _dot(g, w2_ref[c0:c0 + FF_CHUNK, :])
        acc = t if acc is None else acc + t
    out_ref[...] = x + 0.5 * mod_ref[2:3, :] * acc


def _ffn(x, mod, nw, w1, w3, w2, layer, half, seq_len, mix=None):
    n = x.shape[0]
    tm = TOKEN_TILE
    tiles_per_seq = None if seq_len is None else seq_len // tm
    row = lambda i: (i, 0)
    in_specs = [pl.BlockSpec((tm, D_MODEL), row)]
    args = [x]
    n_parts = 0
    if mix is not None:
        parts, (w_out, w_lead) = mix
        n_parts = len(parts)
        in_specs.append(_mod_spec(layer, 1, tiles_per_seq))
        args.append(mod)
        in_specs += [pl.BlockSpec((tm, p.shape[1]), row) for p in parts]
        args += list(parts)
        r0 = 0
        for p in parts:
            rows = p.shape[1]
            in_specs.append(pl.BlockSpec((None, rows, D_MODEL), lambda i, b=r0 // rows: (w_lead, b, 0),
                                         pipeline_mode=pl.Buffered(1)))
            args.append(w_out)
            r0 += rows
    sub = 2 * half
    in_specs += [
        _mod_spec(layer, sub, tiles_per_seq),
        _resident_spec((1, D_MODEL), layer, sub),
        _resident_spec((D_MODEL, D_FF), layer, half),
        _resident_spec((D_MODEL, D_FF), layer, half),
        _resident_spec((D_FF, D_MODEL), layer, half),
    ]
    args += [mod, nw, w1, w3, w2]
    return pl.pallas_call(
        functools.partial(_ffn_kernel, n_parts=n_parts),
        grid=(n // tm,),
        in_specs=in_specs,
        out_specs=pl.BlockSpec((tm, D_MODEL), row),
        out_shape=jax.ShapeDtypeStruct((n, D_MODEL), f32),
        compiler_params=_cparams(1),
    )(*args)


def _lane_iota():
    return lax.broadcasted_iota(jnp.int32, (1, LANES), 1)


def _head_rms(z, low):
    sq = z * z
    s_low = jnp.sum(jnp.where(low, sq, 0.0), axis=-1, keepdims=True)
    s_high = jnp.sum(jnp.where(low, 0.0, sq), axis=-1, keepdims=True)
    return jnp.where(low, s_low, s_high) * (1.0 / HEAD_DIM)


def _rope(z, cos, sin_signed, first_half):
    rot = jnp.where(first_half, pltpu.roll(z, LANES - 16, 1), pltpu.roll(z, 16, 1))
    return z * cos + rot * sin_signed


def _proj_even_kernel(*refs, rope, ctx):
    x_ref, mod_ref, nw_ref, w_ref, qn_ref, kn_ref = refs[:6]
    pos = 6
    if rope:
        cos_ref, sin_ref = refs[6:8]
        pos = 8
    aq_ref, ak_ref, av_ref, bq_ref, bk_ref, bv_ref = refs[pos:pos + 6]
    if ctx:
        akf_ref, avf_ref, bkf_ref, bvf_ref = refs[pos + 6:pos + 10]

    h = _modulated_norm(x_ref[...], mod_ref, nw_ref).astype(bf16)
    y = _dot(h, w_ref[...])
    lane = _lane_iota()
    low = lane < HEAD_DIM
    first_half = (lane % 32) < 16
    if rope:
        cos = cos_ref[...]
        sin = sin_ref[...]

    for j in range(A_HEADS // 2):
        z = y[:, j * LANES:(j + 1) * LANES]
        z = z * lax.rsqrt(_head_rms(z, low) + NORM_EPS) * qn_ref[...]
        if rope:
            z = _rope(z, cos, sin, first_half)
        z = z * Q_SCALE
        zr = pltpu.roll(z, HEAD_DIM, 1)
        g = j // 2
        keep = low if g == 0 else jnp.logical_not(low)
        h0 = z if g == 0 else zr
        h1 = zr if g == 0 else z
        aq_ref[:, (2 * j) * LANES:(2 * j + 1) * LANES] = jnp.where(keep, h0, 0.0).astype(bf16)
        aq_ref[:, (2 * j + 1) * LANES:(2 * j + 2) * LANES] = jnp.where(keep, h1, 0.0).astype(bf16)

    o = A_HEADS * HEAD_DIM
    z = y[:, o:o + LANES]
    z = z * lax.rsqrt(_head_rms(z, low) + NORM_EPS) * kn_ref[...]
    if ctx:
        akf_ref[...] = z
    if rope:
        z = _rope(z, cos, sin, first_half)
    ak_ref[...] = z.astype(bf16)
    o += LANES
    z = y[:, o:o + LANES]
    if ctx:
        avf_ref[...] = z
    av_ref[...] = z.astype(bf16)
    o += LANES
    nb = B_HEADS * HEAD_DIM
    bq_ref[...] = (y[:, o:o + nb] * Q_SCALE).astype(bf16)
    z = y[:, o + nb:o + 2 * nb]
    if ctx:
        bkf_ref[...] = z
    bk_ref[...] = z.astype(bf16)
    z = y[:, o + 2 * nb:o + 3 * nb]
    if ctx:
        bvf_ref[...] = z
    bv_ref[...] = z.astype(bf16)


def _proj_even(x, mod, nw, w_in, qn, kn, rope_tabs, layer, ctx):
    n = x.shape[0]
    tm = TOKEN_TILE
    e = layer // 2
    rope = rope_tabs is not None
    tiles_per_seq = None if ctx else DEC_SEQ // tm
    row = lambda i: (i, 0)
    in_specs = [
        pl.BlockSpec((tm, D_MODEL), row),
        _mod_spec(layer, 1, tiles_per_seq),
        _resident_spec((1, D_MODEL), layer, 1),
        _resident_spec((D_MODEL, AB_IN), e),
        _resident_spec((1, LANES), e),
        _resident_spec((1, LANES), e),
    ]
    args = [x, mod, nw, w_in, qn, kn]
    if rope:
        pos_spec = pl.BlockSpec((tm, LANES), lambda i: (i % tiles_per_seq, 0))
        in_specs += [pos_spec, pos_spec]
        args += list(rope_tabs)
    widths = [2 * A_HEADS * HEAD_DIM, LANES, LANES, 512, 512, 512]
    out_shape = [jax.ShapeDtypeStruct((n, w), bf16) for w in widths]
    out_specs = [pl.BlockSpec((tm, w), row) for w in widths]
    if ctx:
        for w in (LANES, LANES, 512, 512):
            out_shape.append(jax.ShapeDtypeStruct((n, w), f32))
            out_specs.append(pl.BlockSpec((tm, w), row))
    return pl.pallas_call(
        functools.partial(_proj_even_kernel, rope=rope, ctx=ctx),
        grid=(n // tm,),
        in_specs=in_specs,
        out_specs=out_specs,
        out_shape=out_shape,
        compiler_params=_cparams(1),
    )(*args)


def _proj_odd_kernel(*refs, rope, ctx):
    x_ref, mod_ref, nw_ref, w_ref = refs[:4]
    pos = 4
    if rope:
        cos_ref, sin_ref = refs[4:6]
        pos = 6
    q_ref, k_ref, v_ref = refs[pos:pos + 3]
    if ctx:
        kf_ref, vf_ref = refs[pos + 3:pos + 5]

    h = _modulated_norm(x_ref[...], mod_ref, nw_ref).astype(bf16)
    y = _dot(h, w_ref[...])
    lane = _lane_iota()
    first_half = (lane % 32) < 16
    for j in range(C_OUT // LANES):
        zq = y[:, j * LANES:(j + 1) * LANES]
        zk = y[:, C_OUT + j * LANES:C_OUT + (j + 1) * LANES]
        if ctx:
            kf_ref[:, j * LANES:(j + 1) * LANES] = zk
        if rope:
            zq = _rope(zq, cos_ref[...], sin_ref[...], first_half)
            zk = _rope(zk, cos_ref[...], sin_ref[...], first_half)
        q_ref[:, j * LANES:(j + 1) * LANES] = (zq * Q_SCALE).astype(bf16)
        k_ref[:, j * LANES:(j + 1) * LANES] = zk.astype(bf16)
    zv = y[:, 2 * C_OUT:3 * C_OUT]
    if ctx:
        vf_ref[...] = zv
    v_ref[...] = zv.astype(bf16)


def _proj_odd(x, mod, nw, w_in, rope_tabs, layer, ctx):
    n = x.shape[0]
    tm = TOKEN_TILE
    rope = rope_tabs is not None
    tiles_per_seq = None if ctx else DEC_SEQ // tm
    row = lambda i: (i, 0)
    in_specs = [
        pl.BlockSpec((tm, D_MODEL), row),
        _mod_spec(layer, 1, tiles_per_seq),
        _resident_spec((1, D_MODEL), layer, 1),
        _resident_spec((D_MODEL, C_IN), layer // 2),
    ]
    args = [x, mod, nw, w_in]
    if rope:
        pos_spec = pl.BlockSpec((tm, LANES), lambda i: (i % tiles_per_seq, 0))
        in_specs += [pos_spec, pos_spec]
        args += list(rope_tabs)
    out_shape = [jax.ShapeDtypeStruct((n, C_OUT), bf16)] * 3
    out_specs = [pl.BlockSpec((tm, C_OUT), row)] * 3
    if ctx:
        out_shape = out_shape + [jax.ShapeDtypeStruct((n, C_OUT), f32)] * 2
        out_specs = out_specs + [pl.BlockSpec((tm, C_OUT), row)] * 2
    return pl.pallas_call(
        functools.partial(_proj_odd_kernel, rope=rope, ctx=ctx),
        grid=(n // tm,),
        in_specs=in_specs,
        out_specs=out_specs,
        out_shape=out_shape,
        compiler_params=_cparams(1),
    )(*args)


def _lane_tile_reduce(x, op):
    acc = x[:, :LANES]
    for i in range(1, x.shape[1] // LANES):
        acc = op(acc, x[:, i * LANES:(i + 1) * LANES])
    return acc


def _store_scores(s_ref, q, keys, bias=None):
    off = 0
    for i, k in enumerate(keys):
        s = _dot_nt(q, k)
        if i == 0 and bias is not None:
            s = s + bias
        s_ref[:, off:off + k.shape[0]] = s
        off += k.shape[0]
    return _lane_tile_reduce(s_ref[...], jnp.maximum).max(axis=-1, keepdims=True)


def _softmax_values(s_ref, m, values):
    l_acc = None
    o = None
    off = 0
    for load, n in values:
        for c0 in range(0, n, KEY_CHUNK):
            c1 = min(c0 + KEY_CHUNK, n)
            p = jnp.exp2(s_ref[:, off + c0:off + c1] - m)
            lt = _lane_tile_reduce(p, jnp.add)
            l_acc = lt if l_acc is None else l_acc + lt
            t = _dot(p.astype(bf16), load(c0, c1))
            o = t if o is None else o + t
        off += n
    return o, l_acc.sum(axis=-1, keepdims=True)


def _run_units(n_units, s_refs, scores, finish):
    m_next = scores(0, s_refs[0])
    for u in range(n_units):
        m = m_next
        if u + 1 < n_units:
            m_next = scores(u + 1, s_refs[(u + 1) % 2])
        finish(u, s_refs[u % 2], m)


def _score_scratch(rows, n_keys):
    return [pltpu.VMEM((rows, n_keys), f32)] * 2


def _cache_spec(layer, width, col_block=lambda *idx: 0):
    return pl.BlockSpec((None, None, PAST_LEN, width), lambda b, *rest: (b, layer, 0, col_block(b, *rest)))


def _attn_a_kernel(*refs, tq, cache):
    if cache:
        q_ref, k_ref, v_ref, kc_ref, vc_ref, o_ref, s0_ref, s1_ref = refs
    else:
        q_ref, k_ref, v_ref, o_ref, s0_ref, s1_ref = refs
    low = _lane_iota() < HEAD_DIM
    group = A_HEADS // A_KV_HEADS
    seq = k_ref.shape[0]
    n_blocks = tq // A_UNIT_Q

    def unit(u):
        g, qb = divmod(u, n_blocks)
        return g, slice(qb * A_UNIT_Q, (qb + 1) * A_UNIT_Q)

    def scores(u, s_ref):
        g, rows = unit(u)
        q = jnp.concatenate(
            [q_ref[rows, (group * g + i) * LANES:(group * g + i + 1) * LANES] for i in range(group)], axis=0)
        keys = [k_ref[...]]
        if cache:
            keys.append(kc_ref[...].astype(bf16))
        return _store_scores(s_ref, q, keys)

    def finish(u, s_ref, m):
        g, rows = unit(u)
        values = [(lambda a, b: v_ref[a:b, :], seq)]
        if cache:
            values.append((lambda a, b: vc_ref[a:b, :].astype(bf16), PAST_LEN))
        o, l = _softmax_values(s_ref, m, values)
        o = o / l
        o_other = pltpu.roll(o, HEAD_DIM, 1)
        o_low, o_high = (o, o_other) if g == 0 else (o_other, o)
        for jj in range(group // 2):
            even = o_low[(2 * jj) * A_UNIT_Q:(2 * jj + 1) * A_UNIT_Q]
            odd = o_high[(2 * jj + 1) * A_UNIT_Q:(2 * jj + 2) * A_UNIT_Q]
            blk = (group // 2) * g + jj
            o_ref[rows, blk * LANES:(blk + 1) * LANES] = jnp.where(low, even, odd).astype(o_ref.dtype)

    _run_units(A_KV_HEADS * n_blocks, (s0_ref, s1_ref), scores, finish)


def _attn_a(q, k, v, kc, vc, layer, batch, seq, tq):
    cache = kc is not None
    nq = seq // tq
    in_specs = [
        pl.BlockSpec((tq, 2 * A_HEADS * HEAD_DIM), lambda b, j: (b * nq + j, 0)),
        pl.BlockSpec((seq, LANES), lambda b, j: (b, 0)),
        pl.BlockSpec((seq, LANES), lambda b, j: (b, 0)),
    ]
    args = [q, k, v]
    if cache:
        in_specs += [_cache_spec(layer, LANES)] * 2
        args += [kc, vc]
    n_keys = seq + (PAST_LEN if cache else 0)
    return pl.pallas_call(
        functools.partial(_attn_a_kernel, tq=tq, cache=cache),
        grid=(batch, nq),
        in_specs=in_specs,
        out_specs=pl.BlockSpec((tq, A_HEADS * HEAD_DIM), lambda b, j: (b * nq + j, 0)),
        out_shape=jax.ShapeDtypeStruct((batch * seq, A_HEADS * HEAD_DIM), bf16),
        scratch_shapes=_score_scratch(A_HEADS // A_KV_HEADS * A_UNIT_Q, n_keys),
        compiler_params=_cparams(2),
    )(*args)


def _pair_lhs(q, low):
    zero = jnp.zeros_like(q)
    return jnp.concatenate([jnp.where(low, q, zero), jnp.where(low, zero, q)], axis=0)


def _attn_pair_ctx_kernel(q_ref, k_ref, v_ref, o_ref, s0_ref, s1_ref, *, tq):
    low = _lane_iota() < HEAD_DIM
    seq = k_ref.shape[0]

    def scores(hp, s_ref):
        sl = slice(hp * LANES, (hp + 1) * LANES)
        return _store_scores(s_ref, _pair_lhs(q_ref[:, sl], low), [k_ref[:, sl]])

    def finish(hp, s_ref, m):
        sl = slice(hp * LANES, (hp + 1) * LANES)
        o, l = _softmax_values(s_ref, m, [(lambda a, b: v_ref[a:b, sl], seq)])
        o = o / l
        o_ref[:, sl] = jnp.where(low, o[:tq], o[tq:]).astype(o_ref.dtype)

    _run_units(B_HEADS // 2, (s0_ref, s1_ref), scores, finish)


def _attn_pair_ctx(q, k, v, batch, seq):
    spec = pl.BlockSpec((seq, B_HEADS * HEAD_DIM), lambda b: (b, 0))
    return pl.pallas_call(
        functools.partial(_attn_pair_ctx_kernel, tq=seq),
        grid=(batch,),
        in_specs=[spec, spec, spec],
        out_specs=spec,
        out_shape=jax.ShapeDtypeStruct((batch * seq, B_HEADS * HEAD_DIM), bf16),
        scratch_shapes=_score_scratch(2 * seq, seq),
        compiler_params=_cparams(1),
    )(q, k, v)


def _nb_band_start(block):
    return jnp.clip(NB_Q_ROWS * block - NA_ROWS // 2, 0, GRID_ROWS - NB_BAND_ROWS)


def _nb_table_index(block):
    return jnp.where(block < 2, block,
                     jnp.where(block >= NB_BLOCKS - 2, block - (NB_BLOCKS - NB_TABLES), 2))


def _attn_nb_kernel(*refs):
    q_ref, k_ref, v_ref, kc_ref, vc_ref = refs[:5]
    bias_refs = refs[5:5 + NB_STEP_BLOCKS]
    o_ref, s0_ref, s1_ref = refs[5 + NB_STEP_BLOCKS:]
    j = pl.program_id(1)
    low = _lane_iota() < HEAD_DIM
    n_pairs = B_HEADS // 2

    def unit(u):
        qb, hp = divmod(u, n_pairs)
        start = pl.multiple_of(_nb_band_start(j * NB_STEP_BLOCKS + qb) * GRID_W, GRID_W)
        return qb, hp, slice(qb * NB_Q, (qb + 1) * NB_Q), slice(hp * LANES, (hp + 1) * LANES), start

    def scores(u, s_ref):
        qb, hp, rows, sl, start = unit(u)
        q = _pair_lhs(q_ref[rows, sl], low)
        bias = jnp.concatenate([bias_refs[qb][2 * hp], bias_refs[qb][2 * hp + 1]], axis=0)
        keys = [k_ref[pl.ds(start, NB_BAND), sl], kc_ref[:, sl].astype(bf16)]
        return _store_scores(s_ref, q, keys, bias)

    def finish(u, s_ref, m):
        qb, hp, rows, sl, start = unit(u)
        values = [(lambda a, b: v_ref[pl.ds(start + a, b - a), sl], NB_BAND),
                  (lambda a, b: vc_ref[a:b, sl].astype(bf16), PAST_LEN)]
        o, l = _softmax_values(s_ref, m, values)
        o = o / l
        o_ref[rows, sl] = jnp.where(low, o[:NB_Q], o[NB_Q:]).astype(o_ref.dtype)

    _run_units(NB_STEP_BLOCKS * n_pairs, (s0_ref, s1_ref), scores, finish)


def _attn_nb(q, k, v, kc, vc, bias, layer):
    width = B_HEADS * HEAD_DIM
    steps = NB_BLOCKS // NB_STEP_BLOCKS
    tq = NB_STEP_BLOCKS * NB_Q
    bias_specs = [
        pl.BlockSpec((B_HEADS, None, NB_Q, NB_BAND),
                     lambda b, j, qb=qb: (layer, _nb_table_index(j * NB_STEP_BLOCKS + qb), 0, 0))
        for qb in range(NB_STEP_BLOCKS)]
    return pl.pallas_call(
        _attn_nb_kernel,
        grid=(DEC_BATCH, steps),
        in_specs=[
            pl.BlockSpec((tq, width), lambda b, j: (b * steps + j, 0)),
            pl.BlockSpec((DEC_SEQ, width), lambda b, j: (b, 0)),
            pl.BlockSpec((DEC_SEQ, width), lambda b, j: (b, 0)),
            _cache_spec(layer, width),
            _cache_spec(layer, width),
        ] + bias_specs,
        out_specs=pl.BlockSpec((tq, width), lambda b, j: (b * steps + j, 0)),
        out_shape=jax.ShapeDtypeStruct((N_LAT, width), bf16),
        scratch_shapes=_score_scratch(2 * NB_Q, NB_BAND + PAST_LEN),
        compiler_params=_cparams(2),
    )(q, k, v, kc, vc, *([bias] * NB_STEP_BLOCKS))


def _nb_row_offsets():
    dr = np.full((NB_TABLES, NB_Q_ROWS, NB_BAND_ROWS), -1, np.int64)
    for t, block in enumerate(NB_TABLE_BLOCKS):
        band_start = int(np.clip(NB_Q_ROWS * block - NA_ROWS // 2, 0, GRID_ROWS - NB_BAND_ROWS))
        for qr in range(NB_Q_ROWS):
            q_row = block * NB_Q_ROWS + qr
            row_start = int(np.clip(q_row - NA_ROWS // 2, 0, GRID_ROWS - NA_ROWS))
            for kr in range(NB_BAND_ROWS):
                k_row = band_start + kr
                if row_start <= k_row < row_start + NA_ROWS:
                    dr[t, qr, kr] = int(np.clip(k_row - q_row + NA_ROWS - 1, 0, 2 * NA_ROWS - 2))
    return dr


def _nb_bias_kernel(r_ref, o_ref):
    lane = lax.broadcasted_iota(jnp.int32, (GRID_W, LANES), 1)
    q_col = lax.broadcasted_iota(jnp.int32, (GRID_W, LANES), 0)
    k_col = lane % GRID_W
    win_c = min(NA_COLS, GRID_W)
    col_start = jnp.clip(q_col - win_c // 2, 0, GRID_W - win_c)
    col_in = (k_col >= col_start) & (k_col < col_start + win_c)
    low = lane < GRID_W
    masked = jnp.full((GRID_W, LANES), MASK_VALUE, f32)
    tiles = []
    for d in range(RPB_ROWS):
        x = jnp.broadcast_to(r_ref[d:d + 1, :], (GRID_W, LANES))
        y = pltpu.roll(x, LANES - (NA_COLS - 1), 1, stride=1, stride_axis=0)
        tiles.append(jnp.where(col_in, y * LOG2E, MASK_VALUE))
    dr = _nb_row_offsets()
    for t in range(NB_TABLES):
        for qr in range(NB_Q_ROWS):
            for kp in range(NB_BAND_ROWS // 2):
                d0, d1 = int(dr[t, qr, 2 * kp]), int(dr[t, qr, 2 * kp + 1])
                left = tiles[d0] if d0 >= 0 else masked
                right = tiles[d1] if d1 >= 0 else masked
                tile = masked if (d0 < 0 and d1 < 0) else jnp.where(low, left, right)
                o_ref[t, qr * GRID_W:(qr + 1) * GRID_W, kp * LANES:(kp + 1) * LANES] = tile


def _nb_bias_tables(b_rpb):
    heads = b_rpb.shape[0] * B_HEADS
    r = b_rpb.reshape(heads, RPB_ROWS, RPB_COLS)
    half = jnp.pad(r, ((0, 0), (0, 16 - RPB_ROWS), (0, GRID_W - RPB_COLS)))
    r2 = jnp.concatenate([half, half], axis=-1)
    return pl.pallas_call(
        _nb_bias_kernel,
        grid=(heads,),
        in_specs=[pl.BlockSpec((None, 16, LANES), lambda h: (h, 0, 0))],
        out_specs=pl.BlockSpec((None, NB_TABLES, NB_Q, NB_BAND), lambda h: (h, 0, 0, 0)),
        out_shape=jax.ShapeDtypeStruct((heads, NB_TABLES, NB_Q, NB_BAND), f32),
        compiler_params=_cparams(1),
    )(r2)


def _attn_diff_kernel(*refs, tq, cache, lam_init, n_heads):
    if cache:
        lam_ref, sub_ref, q_ref, k_ref, v_ref, kc_ref, vc_ref, o_ref, s0_ref, s1_ref = refs
    else:
        lam_ref, sub_ref, q_ref, k_ref, v_ref, o_ref, s0_ref, s1_ref = refs
    low = _lane_iota() < HEAD_DIM
    seq = k_ref.shape[0]
    lp = lam_ref[...]
    lam = (jnp.exp(jnp.sum(lp[0:1] * lp[1:2], axis=-1, keepdims=True))
           - jnp.exp(jnp.sum(lp[2:3] * lp[3:4], axis=-1, keepdims=True)) + lam_init)

    def scores(h, s_ref):
        sl = slice(h * LANES, (h + 1) * LANES)
        keys = [k_ref[:, sl]]
        if cache:
            keys.append(kc_ref[:, sl].astype(bf16))
        return _store_scores(s_ref, _pair_lhs(q_ref[:, sl], low), keys)

    def finish(h, s_ref, m):
        sl = slice(h * LANES, (h + 1) * LANES)
        values = [(lambda a, b: v_ref[a:b, sl], seq)]
        if cache:
            values.append((lambda a, b: vc_ref[a:b, sl].astype(bf16), PAST_LEN))
        o, l = _softmax_values(s_ref, m, values)
        inv = 1.0 / l
        o = o[:tq] * inv[:tq] - o[tq:] * (lam * inv[tq:])
        ms = jnp.mean(o * o, axis=-1, keepdims=True)
        o = o * lax.rsqrt(ms + NORM_EPS) * sub_ref[...] * (1.0 - lam_init)
        o_ref[:, sl] = o.astype(o_ref.dtype)

    _run_units(n_heads, (s0_ref, s1_ref), scores, finish)


def _attn_diff(lam_p, subln, q, k, v, kc, vc, layer, batch, seq, tq, heads_per_step, lam_init):
    cache = kc is not None
    nq = seq // tq
    gw = heads_per_step * LANES
    in_specs = [
        pl.BlockSpec((None, 4, LANES), lambda b, g, j: (layer, 0, 0)),
        pl.BlockSpec((None, 1, LANES), lambda b, g, j: (layer, 0, 0)),
        pl.BlockSpec((tq, gw), lambda b, g, j: (b * nq + j, g)),
        pl.BlockSpec((seq, gw), lambda b, g, j: (b, g)),
        pl.BlockSpec((seq, gw), lambda b, g, j: (b, g)),
    ]
    args = [lam_p, subln, q, k, v]
    if cache:
        in_specs += [_cache_spec(layer, gw, lambda b, g, j: g)] * 2
        args += [kc, vc]
    n_keys = seq + (PAST_LEN if cache else 0)
    return pl.pallas_call(
        functools.partial(_attn_diff_kernel, tq=tq, cache=cache, lam_init=lam_init, n_heads=heads_per_step),
        grid=(batch, C_HEADS // heads_per_step, nq),
        in_specs=in_specs,
        out_specs=pl.BlockSpec((tq, gw), lambda b, g, j: (b * nq + j, g)),
        out_shape=jax.ShapeDtypeStruct((batch * seq, C_OUT), bf16),
        scratch_shapes=_score_scratch(2 * tq, n_keys),
        compiler_params=_cparams(3),
    )(*args)


def _final_norm_kernel(x_ref, w_ref, o_ref):
    x = x_ref[...]
    ms = jnp.mean(x * x, axis=-1, keepdims=True)
    o_ref[...] = x * lax.rsqrt(ms + NORM_EPS) * w_ref[...]


def _final_norm(x, w):
    n = x.shape[0]
    tm = TOKEN_TILE
    return pl.pallas_call(
        _final_norm_kernel,
        grid=(n // tm,),
        in_specs=[pl.BlockSpec((tm, D_MODEL), lambda i: (i, 0)), _resident_spec((1, D_MODEL))],
        out_specs=pl.BlockSpec((tm, D_MODEL), lambda i: (i, 0)),
        out_shape=jax.ShapeDtypeStruct((n, D_MODEL), f32),
        compiler_params=_cparams(1),
    )(x, w)


def _rope_tables():
    t = jnp.arange(DEC_SEQ)
    row = (t // GRID_W).astype(f32)
    col = (t % GRID_W).astype(f32)
    n_freq = HEAD_DIM // 4
    inv_freq = ROPE_THETA ** (-jnp.arange(n_freq, dtype=f32) / n_freq)
    ang_r = row[:, None] * inv_freq
    ang_c = col[:, None] * inv_freq
    ang = jnp.concatenate([ang_r, ang_r, ang_c, ang_c], axis=-1)
    cos = jnp.cos(ang)
    sin = jnp.sin(ang)
    sign = jnp.asarray(np.tile(np.repeat(np.array([-1.0, 1.0], np.float32), n_freq), 2))
    return jnp.tile(cos, (1, 2)), jnp.tile(sin * sign, (1, 2))


def kernel(x_prompt, x_sample, cache_a_k, cache_a_v, cache_b_k, cache_b_v, cache_c_k, cache_c_v, c, c_ctx,
           w_mod, b_mod, norm_w, ffn_w1, ffn_w3, ffn_w2, w_in_ab, w_out_ab, a_q_norm, a_k_norm, b_rpb,
           w_in_c, w_out_c, c_lambda, c_subln, final_norm):
    lam_inits = [0.8 - 0.6 * math.exp(-0.3 * l) for l in range(DEPTH)]
    n_even = w_in_ab.shape[0]
    n_odd = w_in_c.shape[0]

    mod = _modulation(c, c_ctx, w_mod, b_mod)
    w1 = ffn_w1.astype(bf16)
    w3 = ffn_w3.astype(bf16)
    w2 = ffn_w2.astype(bf16)
    w_in_ab16 = w_in_ab.astype(bf16)
    w_out_ab16 = w_out_ab.astype(bf16)
    w_in_c16 = w_in_c.astype(bf16)
    w_out_c16 = w_out_c.astype(bf16)
    nw = norm_w.reshape(DEPTH, 3, 1, D_MODEL)
    qn = jnp.tile(a_q_norm, (1, 2)).reshape(n_even, 1, LANES)
    kn = jnp.tile(a_k_norm, (1, 2)).reshape(n_even, 1, LANES)
    lam_p = jnp.pad(c_lambda, ((0, 0), (0, 0), (0, LANES - HEAD_DIM)))
    subln = c_subln.reshape(n_odd, 1, 2 * HEAD_DIM)
    rope_tabs = _rope_tables()
    nb_bias = _nb_bias_tables(b_rpb)

    ca_k = cache_a_k.reshape(DEC_BATCH, n_even, PAST_LEN, A_KV_HEADS * HEAD_DIM)
    ca_v = cache_a_v.reshape(DEC_BATCH, n_even, PAST_LEN, A_KV_HEADS * HEAD_DIM)
    cb_k = cache_b_k.reshape(DEC_BATCH, n_even, PAST_LEN, B_HEADS * HEAD_DIM)
    cb_v = cache_b_v.reshape(DEC_BATCH, n_even, PAST_LEN, B_HEADS * HEAD_DIM)
    cc_k = cache_c_k.reshape(DEC_BATCH, n_odd, PAST_LEN, C_OUT)
    cc_v = cache_c_v.reshape(DEC_BATCH, n_odd, PAST_LEN, C_OUT)

    xs = x_sample.reshape(N_LAT, D_MODEL)
    xp = x_prompt.reshape(N_CTX, D_MODEL)
    new = {name: [] for name in ("ak", "av", "bk", "bv", "ck", "cv")}

    def ffn(x, l, half, seq_len, mix=None):
        return _ffn(x, mod, nw, w1, w3, w2, l, half, seq_len, mix)

    for l in range(DEPTH):
        xs = ffn(xs, l, 0, DEC_SEQ)
        xp = ffn(xp, l, 0, None)
        if l % 2 == 0:
            e = l // 2
            aq, ak, av, bq, bk, bv = _proj_even(xs, mod, nw, w_in_ab16, qn, kn, rope_tabs, l, False)
            oa = _attn_a(aq, ak, av, ca_k, ca_v, e, DEC_BATCH, DEC_SEQ, 2 * A_UNIT_Q)
            ob = _attn_nb(bq, bk, bv, cb_k, cb_v, nb_bias, e)
            xs = ffn(xs, l, 1, DEC_SEQ, ((oa, ob), (w_out_ab16, e)))

            aq, ak, av, bq, bk, bv, akf, avf, bkf, bvf = _proj_even(xp, mod, nw, w_in_ab16, qn, kn, None, l,
                                                                    True)
            oa = _attn_a(aq, ak, av, None, None, e, BATCH, SEQ, SEQ)
            ob = _attn_pair_ctx(bq, bk, bv, BATCH, SEQ)
            xp = ffn(xp, l, 1, None, ((oa, ob), (w_out_ab16, e)))
            new["ak"].append(akf); new["av"].append(avf); new["bk"].append(bkf); new["bv"].append(bvf)
        else:
            o = l // 2
            q, k, v = _proj_odd(xs, mod, nw, w_in_c16, rope_tabs, l, False)
            oc = _attn_diff(lam_p, subln, q, k, v, cc_k, cc_v, o, DEC_BATCH, DEC_SEQ, 256, 4, lam_inits[l])
            xs = ffn(xs, l, 1, DEC_SEQ, ((oc,), (w_out_c16, o)))

            q, k, v, kf, vf = _proj_odd(xp, mod, nw, w_in_c16, None, l, True)
            oc = _attn_diff(lam_p, subln, q, k, v, None, None, o, BATCH, SEQ, SEQ, C_HEADS, lam_inits[l])
            xp = ffn(xp, l, 1, None, ((oc,), (w_out_c16, o)))
            new["ck"].append(kf); new["cv"].append(vf)

    fn = final_norm.reshape(1, D_MODEL)
    y_sample = _final_norm(xs, fn).reshape(DEC_BATCH, DEC_SEQ, D_MODEL)
    y_prompt = _final_norm(xp, fn).reshape(BATCH, SEQ, D_MODEL)

    def stack(parts, heads, dim):
        return jnp.stack([p.reshape(BATCH, SEQ, heads, dim) for p in parts], axis=1)

    return (y_prompt, y_sample,
            stack(new["ak"], A_KV_HEADS, HEAD_DIM), stack(new["av"], A_KV_HEADS, HEAD_DIM),
            stack(new["bk"], B_HEADS, HEAD_DIM), stack(new["bv"], B_HEADS, HEAD_DIM),
            stack(new["ck"], C_HEADS, 2 * HEAD_DIM), stack(new["cv"], C_HEADS, 2 * HEAD_DIM))
```

```python
import functools
import math

import numpy as np
import jax
import jax.numpy as jnp
from jax import lax
from jax.experimental import pallas as pl
from jax.experimental.pallas import tpu as pltpu

D_MODEL = 1024
BATCH = 16
SEQ = 256
DEPTH = 4
DEC_BATCH = 8
DEC_SEQ = 2048
PAST_LEN = 512
GRID_W = 64
GRID_ROWS = DEC_SEQ // GRID_W
HEAD_DIM = 64
A_HEADS = 8
A_KV_HEADS = 2
B_HEADS = 8
C_HEADS = 8
NA_ROWS = 8
NA_COLS = 16
D_FF = 2816
N_MOD = 9
ROPE_THETA = 10000.0
NORM_EPS = 1e-6
MASK_VALUE = -1e30
AB_IN = (A_HEADS + 2 * A_KV_HEADS + 3 * B_HEADS) * HEAD_DIM
AB_OUT = (A_HEADS + B_HEADS) * HEAD_DIM
C_IN = 3 * C_HEADS * 2 * HEAD_DIM
C_OUT = C_HEADS * 2 * HEAD_DIM

LANES = 128
FF_CHUNK = 256
KEY_CHUNK = 512
MOD_ROWS = 16
CTX_MOD_ROW = DEC_BATCH
VMEM_LIMIT = 52 * 1024 * 1024
TOKEN_TILE = 512

N_LAT = DEC_BATCH * DEC_SEQ
N_CTX = BATCH * SEQ

LOG2E = math.log2(math.e)
Q_SCALE = HEAD_DIM ** -0.5 * LOG2E

A_UNIT_Q = 128

NB_Q_ROWS = 2
NB_Q = NB_Q_ROWS * GRID_W
NB_BAND_ROWS = 10
NB_BAND = NB_BAND_ROWS * GRID_W
NB_BLOCKS = DEC_SEQ // NB_Q
NB_STEP_BLOCKS = 2
NB_TABLE_BLOCKS = (0, 1, 2, NB_BLOCKS - 2, NB_BLOCKS - 1)
NB_TABLES = len(NB_TABLE_BLOCKS)
RPB_ROWS = 2 * NA_ROWS - 1
RPB_COLS = 2 * NA_COLS - 1

bf16 = jnp.bfloat16
f32 = jnp.float32


def _dot(a, b):
    return jnp.dot(a, b, preferred_element_type=f32)


def _dot_nt(a, b):
    return lax.dot_general(a, b, (((1,), (1,)), ((), ())), preferred_element_type=f32)


def _cparams(n_axes):
    return pltpu.CompilerParams(dimension_semantics=("arbitrary",) * n_axes,
                                vmem_limit_bytes=VMEM_LIMIT)


def _resident_spec(shape, *lead):
    block = (None,) * len(lead) + tuple(shape)
    index = tuple(lead) + (0,) * len(shape)
    return pl.BlockSpec(block, lambda *_: index, pipeline_mode=pl.Buffered(1))


def _modulated_norm(x, mod_ref, nw_ref):
    ms = jnp.mean(x * x, axis=-1, keepdims=True)
    y = x * lax.rsqrt(ms + NORM_EPS) * nw_ref[...]
    return y * (1.0 + mod_ref[1:2, :]) + mod_ref[0:1, :]


def _mod_kernel(c_ref, w_ref, b_ref, o_ref):
    c = c_ref[...]
    s = (c * jax.nn.sigmoid(c)).astype(bf16)
    o_ref[...] = _dot(s, w_ref[...].astype(bf16)) + b_ref[...]


def _modulation(c, c_ctx, w_mod, b_mod):
    cc = jnp.zeros((MOD_ROWS, D_MODEL), f32)
    cc = cc.at[:DEC_BATCH].set(c).at[CTX_MOD_ROW].set(c_ctx)
    out = pl.pallas_call(
        _mod_kernel,
        grid=(DEPTH, N_MOD),
        in_specs=[
            pl.BlockSpec((MOD_ROWS, D_MODEL), lambda l, k: (0, 0)),
            pl.BlockSpec((None, D_MODEL, D_MODEL), lambda l, k: (l, 0, k)),
            pl.BlockSpec((None, None, 1, D_MODEL), lambda l, k: (l, k, 0, 0)),
        ],
        out_specs=pl.BlockSpec((None, None, MOD_ROWS, D_MODEL), lambda l, k: (l, k, 0, 0)),
        out_shape=jax.ShapeDtypeStruct((DEPTH, N_MOD, MOD_ROWS, D_MODEL), f32),
        compiler_params=_cparams(2),
    )(cc, w_mod, b_mod.reshape(DEPTH, N_MOD, 1, D_MODEL))
    return out.reshape(DEPTH, 3, 3, MOD_ROWS, D_MODEL).transpose(0, 1, 3, 2, 4)


def _mod_spec(layer, sub, tiles_per_seq):
    if tiles_per_seq is None:
        return pl.BlockSpec((None, None, None, 3, D_MODEL),
                            lambda i, *_: (layer, sub, CTX_MOD_ROW, 0, 0))
    return pl.BlockSpec((None, None, None, 3, D_MODEL),
                        lambda i, *_: (layer, sub, i // tiles_per_seq, 0, 0))


N_FFN_REFS = 5


def _swiglu_step(x, mod_ref, nw_ref, w1_ref, w3_ref, w2_ref):
    h = _modulated_norm(x, mod_ref, nw_ref).astype(bf16)
    acc = None
    for c0 in range(0, D_FF, FF_CHUNK):
        a1 = _dot(h, w1_ref[:, c0:c0 + FF_CHUNK])
        a3 = _dot(h, w3_ref[:, c0:c0 + FF_CHUNK])
        g = (a1 * jax.nn.sigmoid(a1) * a3).astype(bf16)
        t = _dot(g, w2_ref[c0:c0 + FF_CHUNK, :])
        acc = t if acc is None else acc + t
    return x + 0.5 * mod_ref[2:3, :] * acc


def _ffn_specs(mod, nw, w1, w3, w2, layer, half, tiles_per_seq):
    sub = 2 * half
    specs = [
        _mod_spec(layer, sub, tiles_per_seq),
        _resident_spec((1, D_MODEL), layer, sub),
        _resident_spec((D_MODEL, D_FF), layer, half),
        _resident_spec((D_MODEL, D_FF), layer, half),
        _resident_spec((D_FF, D_MODEL), layer, half),
    ]
    return specs, [mod, nw, w1, w3, w2]


def _mix_ffn_kernel(*refs, n_parts, final):
    x_ref, mix_mod_ref = refs[:2]
    o_refs = refs[2:2 + n_parts]
    wo_refs = refs[2 + n_parts:2 + 2 * n_parts]
    pos = 2 + 2 * n_parts
    y = _dot(o_refs[0][...], wo_refs[0][...])
    for o_ref, w_ref in zip(o_refs[1:], wo_refs[1:]):
        y = y + _dot(o_ref[...], w_ref[...])
    x = x_ref[...] + mix_mod_ref[2:3, :] * y
    x = _swiglu_step(x, *refs[pos:pos + N_FFN_REFS])
    pos += N_FFN_REFS
    if final:
        fn_ref, out_ref = refs[pos:]
        ms = jnp.mean(x * x, axis=-1, keepdims=True)
        x = x * lax.rsqrt(ms + NORM_EPS) * fn_ref[...]
    else:
        out_ref, = refs[pos:]
    out_ref[...] = x


def _mix_ffn(x, mod, nw, w1, w3, w2, layer, seq_len, parts, w_out, w_lead, final_w=None):
    n = x.shape[0]
    tm = TOKEN_TILE
    tiles_per_seq = None if seq_len is None else seq_len // tm
    row = lambda i: (i, 0)
    in_specs = [pl.BlockSpec((tm, D_MODEL), row), _mod_spec(layer, 1, tiles_per_seq)]
    args = [x, mod]
    in_specs += [pl.BlockSpec((tm, p.shape[1]), row) for p in parts]
    args += list(parts)
    r0 = 0
    for p in parts:
        rows = p.shape[1]
        in_specs.append(pl.BlockSpec((None, rows, D_MODEL), lambda i, b=r0 // rows: (w_lead, b, 0),
                                     pipeline_mode=pl.Buffered(1)))
        args.append(w_out)
        r0 += rows
    ffn_specs, ffn_args = _ffn_specs(mod, nw, w1, w3, w2, layer, 1, tiles_per_seq)
    in_specs += ffn_specs
    args += ffn_args
    if final_w is not None:
        in_specs.append(_resident_spec((1, D_MODEL)))
        args.append(final_w)
    return pl.pallas_call(
        functools.partial(_mix_ffn_kernel, n_parts=len(parts), final=final_w is not None),
        grid=(n // tm,),
        in_specs=in_specs,
        out_specs=pl.BlockSpec((tm, D_MODEL), row),
        out_shape=jax.ShapeDtypeStruct((n, D_MODEL), f32),
        compiler_params=_cparams(1),
    )(*args)


def _lane_iota():
    return lax.broadcasted_iota(jnp.int32, (1, LANES), 1)


def _head_rms(z, low):
    sq = z * z
    s_low = jnp.sum(jnp.where(low, sq, 0.0), axis=-1, keepdims=True)
    s_high = jnp.sum(jnp.where(low, 0.0, sq), axis=-1, keepdims=True)
    return jnp.where(low, s_low, s_high) * (1.0 / HEAD_DIM)


def _rope(z, cos, sin_signed, first_half):
    rot = jnp.where(first_half, pltpu.roll(z, LANES - 16, 1), pltpu.roll(z, 16, 1))
    return z * cos + rot * sin_signed


def _cache_rows(z):
    return z.reshape(TOKEN_TILE // SEQ, SEQ, z.shape[1])


def _proj_even_kernel(*refs, rope, ctx, n_alias):
    x_ref = refs[0]
    pos = 1 + N_FFN_REFS
    x = _swiglu_step(x_ref[...], *refs[1:pos])
    mod_ref, nw_ref, w_ref, qn_ref, kn_ref = refs[pos:pos + 5]
    pos += 5
    if rope:
        cos_ref, sin_ref = refs[pos:pos + 2]
        pos += 2
    pos += n_alias
    xo_ref, aq_ref, ak_ref, av_ref, bq_ref, bk_ref, bv_ref = refs[pos:pos + 7]
    if ctx:
        akf_ref, avf_ref, bkf_ref, bvf_ref = refs[pos + 7:pos + 11]
    xo_ref[...] = x

    h = _modulated_norm(x, mod_ref, nw_ref).astype(bf16)
    y = _dot(h, w_ref[...])
    lane = _lane_iota()
    low = lane < HEAD_DIM
    first_half = (lane % 32) < 16
    if rope:
        cos = cos_ref[...]
        sin = sin_ref[...]

    for j in range(A_HEADS // 2):
        z = y[:, j * LANES:(j + 1) * LANES]
        z = z * lax.rsqrt(_head_rms(z, low) + NORM_EPS) * qn_ref[...]
        if rope:
            z = _rope(z, cos, sin, first_half)
        z = z * Q_SCALE
        zr = pltpu.roll(z, HEAD_DIM, 1)
        g = j // 2
        keep = low if g == 0 else jnp.logical_not(low)
        h0 = z if g == 0 else zr
        h1 = zr if g == 0 else z
        aq_ref[:, (2 * j) * LANES:(2 * j + 1) * LANES] = jnp.where(keep, h0, 0.0).astype(bf16)
        aq_ref[:, (2 * j + 1) * LANES:(2 * j + 2) * LANES] = jnp.where(keep, h1, 0.0).astype(bf16)

    o = A_HEADS * HEAD_DIM
    z = y[:, o:o + LANES]
    z = z * lax.rsqrt(_head_rms(z, low) + NORM_EPS) * kn_ref[...]
    if ctx:
        akf_ref[...] = _cache_rows(z)
    if rope:
        z = _rope(z, cos, sin, first_half)
    ak_ref[...] = z.astype(bf16)
    o += LANES
    z = y[:, o:o + LANES]
    if ctx:
        avf_ref[...] = _cache_rows(z)
    av_ref[...] = z.astype(bf16)
    o += LANES
    nb = B_HEADS * HEAD_DIM
    bq_ref[...] = (y[:, o:o + nb] * Q_SCALE).astype(bf16)
    z = y[:, o + nb:o + 2 * nb]
    if ctx:
        bkf_ref[...] = _cache_rows(z)
    bk_ref[...] = z.astype(bf16)
    z = y[:, o + 2 * nb:o + 3 * nb]
    if ctx:
        bvf_ref[...] = _cache_rows(z)
    bv_ref[...] = z.astype(bf16)


def _new_cache_outputs(widths, n_layers, slot, prev):
    seqs = TOKEN_TILE // SEQ
    shapes = [jax.ShapeDtypeStruct((BATCH, n_layers, SEQ, w), f32) for w in widths]
    specs = [pl.BlockSpec((seqs, None, SEQ, w), lambda i: (i, slot, 0, 0)) for w in widths]
    alias_specs = [pl.BlockSpec(memory_space=pl.ANY)] * (len(widths) if prev is not None else 0)
    return shapes, specs, alias_specs


def _ffn_proj_even(x, mod, nw, w1, w3, w2, w_in, qn, kn, rope_tabs, layer, ctx, prev_caches=None):
    n = x.shape[0]
    tm = TOKEN_TILE
    e = layer // 2
    rope = rope_tabs is not None
    tiles_per_seq = None if ctx else DEC_SEQ // tm
    row = lambda i: (i, 0)
    ffn_specs, ffn_args = _ffn_specs(mod, nw, w1, w3, w2, layer, 0, tiles_per_seq)
    in_specs = [pl.BlockSpec((tm, D_MODEL), row)] + ffn_specs + [
        _mod_spec(layer, 1, tiles_per_seq),
        _resident_spec((1, D_MODEL), layer, 1),
        _resident_spec((D_MODEL, AB_IN), e),
        _resident_spec((1, LANES), e),
        _resident_spec((1, LANES), e),
    ]
    args = [x] + ffn_args + [mod, nw, w_in, qn, kn]
    if rope:
        pos_spec = pl.BlockSpec((tm, LANES), lambda i: (i % tiles_per_seq, 0))
        in_specs += [pos_spec, pos_spec]
        args += list(rope_tabs)
    widths = [2 * A_HEADS * HEAD_DIM, LANES, LANES, 512, 512, 512]
    out_shape = [jax.ShapeDtypeStruct((n, D_MODEL), f32)] + [jax.ShapeDtypeStruct((n, w), bf16) for w in widths]
    out_specs = [pl.BlockSpec((tm, D_MODEL), row)] + [pl.BlockSpec((tm, w), row) for w in widths]
    aliases = {}
    n_alias = 0
    if ctx:
        shapes, specs, alias_specs = _new_cache_outputs((LANES, LANES, 512, 512), w_in.shape[0], e, prev_caches)
        n_alias = len(alias_specs)
        aliases = {len(args) + k: len(out_shape) + k for k in range(n_alias)}
        in_specs += alias_specs
        args += list(prev_caches or ())
        out_shape += shapes
        out_specs += specs
    return pl.pallas_call(
        functools.partial(_proj_even_kernel, rope=rope, ctx=ctx, n_alias=n_alias),
        grid=(n // tm,),
        in_specs=in_specs,
        out_specs=out_specs,
        out_shape=out_shape,
        input_output_aliases=aliases,
        compiler_params=_cparams(1),
    )(*args)


def _proj_odd_kernel(*refs, rope, ctx, n_alias):
    x_ref = refs[0]
    pos = 1 + N_FFN_REFS
    x = _swiglu_step(x_ref[...], *refs[1:pos])
    mod_ref, nw_ref, w_ref = refs[pos:pos + 3]
    pos += 3
    if rope:
        cos_ref, sin_ref = refs[pos:pos + 2]
        pos += 2
    pos += n_alias
    xo_ref, q_ref, k_ref, v_ref = refs[pos:pos + 4]
    if ctx:
        kf_ref, vf_ref = refs[pos + 4:pos + 6]
    xo_ref[...] = x

    h = _modulated_norm(x, mod_ref, nw_ref).astype(bf16)
    y = _dot(h, w_ref[...])
    lane = _lane_iota()
    first_half = (lane % 32) < 16
    for j in range(C_OUT // LANES):
        zq = y[:, j * LANES:(j + 1) * LANES]
        zk = y[:, C_OUT + j * LANES:C_OUT + (j + 1) * LANES]
        if ctx:
            kf_ref[:, :, j * LANES:(j + 1) * LANES] = _cache_rows(zk)
        if rope:
            zq = _rope(zq, cos_ref[...], sin_ref[...], first_half)
            zk = _rope(zk, cos_ref[...], sin_ref[...], first_half)
        q_ref[:, j * LANES:(j + 1) * LANES] = (zq * Q_SCALE).astype(bf16)
        k_ref[:, j * LANES:(j + 1) * LANES] = zk.astype(bf16)
    zv = y[:, 2 * C_OUT:3 * C_OUT]
    if ctx:
        vf_ref[...] = _cache_rows(zv)
    v_ref[...] = zv.astype(bf16)


def _ffn_proj_odd(x, mod, nw, w1, w3, w2, w_in, rope_tabs, layer, ctx, prev_caches=None):
    n = x.shape[0]
    tm = TOKEN_TILE
    o = layer // 2
    rope = rope_tabs is not None
    tiles_per_seq = None if ctx else DEC_SEQ // tm
    row = lambda i: (i, 0)
    ffn_specs, ffn_args = _ffn_specs(mod, nw, w1, w3, w2, layer, 0, tiles_per_seq)
    in_specs = [pl.BlockSpec((tm, D_MODEL), row)] + ffn_specs + [
        _mod_spec(layer, 1, tiles_per_seq),
        _resident_spec((1, D_MODEL), layer, 1),
        _resident_spec((D_MODEL, C_IN), o),
    ]
    args = [x] + ffn_args + [mod, nw, w_in]
    if rope:
        pos_spec = pl.BlockSpec((tm, LANES), lambda i: (i % tiles_per_seq, 0))
        in_specs += [pos_spec, pos_spec]
        args += list(rope_tabs)
    out_shape = [jax.ShapeDtypeStruct((n, D_MODEL), f32)] + [jax.ShapeDtypeStruct((n, C_OUT), bf16)] * 3
    out_specs = [pl.BlockSpec((tm, D_MODEL), row)] + [pl.BlockSpec((tm, C_OUT), row)] * 3
    aliases = {}
    n_alias = 0
    if ctx:
        shapes, specs, alias_specs = _new_cache_outputs((C_OUT, C_OUT), w_in.shape[0], o, prev_caches)
        n_alias = len(alias_specs)
        aliases = {len(args) + k: len(out_shape) + k for k in range(n_alias)}
        in_specs += alias_specs
        args += list(prev_caches or ())
        out_shape += shapes
        out_specs += specs
    return pl.pallas_call(
        functools.partial(_proj_odd_kernel, rope=rope, ctx=ctx, n_alias=n_alias),
        grid=(n // tm,),
        in_specs=in_specs,
        out_specs=out_specs,
        out_shape=out_shape,
        input_output_aliases=aliases,
        compiler_params=_cparams(1),
    )(*args)


def _lane_tile_reduce(x, op):
    acc = x[:, :LANES]
    for i in range(1, x.shape[1] // LANES):
        acc = op(acc, x[:, i * LANES:(i + 1) * LANES])
    return acc


def _store_scores(s_ref, q, keys, bias=None):
    off = 0
    for i, k in enumerate(keys):
        s = _dot_nt(q, k)
        if i == 0 and bias is not None:
            s = s + bias
        s_ref[:, off:off + k.shape[0]] = s
        off += k.shape[0]
    return _lane_tile_reduce(s_ref[...], jnp.maximum).max(axis=-1, keepdims=True)


def _softmax_values(s_ref, m, values):
    l_acc = None
    o = None
    off = 0
    for load, n in values:
        for c0 in range(0, n, KEY_CHUNK):
            c1 = min(c0 + KEY_CHUNK, n)
            p = jnp.exp2(s_ref[:, off + c0:off + c1] - m)
            lt = _lane_tile_reduce(p, jnp.add)
            l_acc = lt if l_acc is None else l_acc + lt
            t = _dot(p.astype(bf16), load(c0, c1))
            o = t if o is None else o + t
        off += n
    return o, l_acc.sum(axis=-1, keepdims=True)


def _run_units(n_units, s_refs, scores, finish):
    m_next = scores(0, s_refs[0])
    for u in range(n_units):
        m = m_next
        if u + 1 < n_units:
            m_next = scores(u + 1, s_refs[(u + 1) % 2])
        finish(u, s_refs[u % 2], m)


def _score_scratch(rows, n_keys):
    return [pltpu.VMEM((rows, n_keys), f32)] * 2


def _cache_spec(layer, width, col_block=lambda *idx: 0):
    return pl.BlockSpec((None, None, PAST_LEN, width), lambda b, *rest: (b, layer, 0, col_block(b, *rest)))


def _attn_a_kernel(*refs, tq, cache):
    if cache:
        q_ref, k_ref, v_ref, kc_ref, vc_ref, o_ref, s0_ref, s1_ref = refs
    else:
        q_ref, k_ref, v_ref, o_ref, s0_ref, s1_ref = refs
    low = _lane_iota() < HEAD_DIM
    group = A_HEADS // A_KV_HEADS
    seq = k_ref.shape[0]
    n_blocks = tq // A_UNIT_Q

    def unit(u):
        g, qb = divmod(u, n_blocks)
        return g, slice(qb * A_UNIT_Q, (qb + 1) * A_UNIT_Q)

    def scores(u, s_ref):
        g, rows = unit(u)
        q = jnp.concatenate(
            [q_ref[rows, (group * g + i) * LANES:(group * g + i + 1) * LANES] for i in range(group)], axis=0)
        keys = [k_ref[...]]
        if cache:
            keys.append(kc_ref[...].astype(bf16))
        return _store_scores(s_ref, q, keys)

    def finish(u, s_ref, m):
        g, rows = unit(u)
        values = [(lambda a, b: v_ref[a:b, :], seq)]
        if cache:
            values.append((lambda a, b: vc_ref[a:b, :].astype(bf16), PAST_LEN))
        o, l = _softmax_values(s_ref, m, values)
        o = o / l
        o_other = pltpu.roll(o, HEAD_DIM, 1)
        o_low, o_high = (o, o_other) if g == 0 else (o_other, o)
        for jj in range(group // 2):
            even = o_low[(2 * jj) * A_UNIT_Q:(2 * jj + 1) * A_UNIT_Q]
            odd = o_high[(2 * jj + 1) * A_UNIT_Q:(2 * jj + 2) * A_UNIT_Q]
            blk = (group // 2) * g + jj
            o_ref[rows, blk * LANES:(blk + 1) * LANES] = jnp.where(low, even, odd).astype(o_ref.dtype)

    _run_units(A_KV_HEADS * n_blocks, (s0_ref, s1_ref), scores, finish)


def _attn_a(q, k, v, kc, vc, layer, batch, seq, tq):
    cache = kc is not None
    nq = seq // tq
    in_specs = [
        pl.BlockSpec((tq, 2 * A_HEADS * HEAD_DIM), lambda b, j: (b * nq + j, 0)),
        pl.BlockSpec((seq, LANES), lambda b, j: (b, 0)),
        pl.BlockSpec((seq, LANES), lambda b, j: (b, 0)),
    ]
    args = [q, k, v]
    if cache:
        in_specs += [_cache_spec(layer, LANES)] * 2
        args += [kc, vc]
    n_keys = seq + (PAST_LEN if cache else 0)
    return pl.pallas_call(
        functools.partial(_attn_a_kernel, tq=tq, cache=cache),
        grid=(batch, nq),
        in_specs=in_specs,
        out_specs=pl.BlockSpec((tq, A_HEADS * HEAD_DIM), lambda b, j: (b * nq + j, 0)),
        out_shape=jax.ShapeDtypeStruct((batch * seq, A_HEADS * HEAD_DIM), bf16),
        scratch_shapes=_score_scratch(A_HEADS // A_KV_HEADS * A_UNIT_Q, n_keys),
        compiler_params=_cparams(2),
    )(*args)


def _pair_lhs(q, low):
    zero = jnp.zeros_like(q)
    return jnp.concatenate([jnp.where(low, q, zero), jnp.where(low, zero, q)], axis=0)


def _attn_pair_ctx_kernel(q_ref, k_ref, v_ref, o_ref, s0_ref, s1_ref, *, tq):
    low = _lane_iota() < HEAD_DIM
    seq = k_ref.shape[0]

    def scores(hp, s_ref):
        sl = slice(hp * LANES, (hp + 1) * LANES)
        return _store_scores(s_ref, _pair_lhs(q_ref[:, sl], low), [k_ref[:, sl]])

    def finish(hp, s_ref, m):
        sl = slice(hp * LANES, (hp + 1) * LANES)
        o, l = _softmax_values(s_ref, m, [(lambda a, b: v_ref[a:b, sl], seq)])
        o = o / l
        o_ref[:, sl] = jnp.where(low, o[:tq], o[tq:]).astype(o_ref.dtype)

    _run_units(B_HEADS // 2, (s0_ref, s1_ref), scores, finish)


def _attn_pair_ctx(q, k, v, batch, seq):
    spec = pl.BlockSpec((seq, B_HEADS * HEAD_DIM), lambda b: (b, 0))
    return pl.pallas_call(
        functools.partial(_attn_pair_ctx_kernel, tq=seq),
        grid=(batch,),
        in_specs=[spec, spec, spec],
        out_specs=spec,
        out_shape=jax.ShapeDtypeStruct((batch * seq, B_HEADS * HEAD_DIM), bf16),
        scratch_shapes=_score_scratch(2 * seq, seq),
        compiler_params=_cparams(1),
    )(q, k, v)


def _nb_band_start(block):
    return jnp.clip(NB_Q_ROWS * block - NA_ROWS // 2, 0, GRID_ROWS - NB_BAND_ROWS)


def _nb_table_index(block):
    return jnp.where(block < 2, block,
                     jnp.where(block >= NB_BLOCKS - 2, block - (NB_BLOCKS - NB_TABLES), 2))


def _attn_nb_kernel(*refs):
    q_ref, k_ref, v_ref, kc_ref, vc_ref = refs[:5]
    bias_refs = refs[5:5 + NB_STEP_BLOCKS]
    o_ref, s0_ref, s1_ref = refs[5 + NB_STEP_BLOCKS:]
    j = pl.program_id(1)
    low = _lane_iota() < HEAD_DIM
    n_pairs = B_HEADS // 2

    def unit(u):
        qb, hp = divmod(u, n_pairs)
        start = pl.multiple_of(_nb_band_start(j * NB_STEP_BLOCKS + qb) * GRID_W, GRID_W)
        return qb, hp, slice(qb * NB_Q, (qb + 1) * NB_Q), slice(hp * LANES, (hp + 1) * LANES), start

    def scores(u, s_ref):
        qb, hp, rows, sl, start = unit(u)
        q = _pair_lhs(q_ref[rows, sl], low)
        bias = jnp.concatenate([bias_refs[qb][2 * hp], bias_refs[qb][2 * hp + 1]], axis=0)
        keys = [k_ref[pl.ds(start, NB_BAND), sl], kc_ref[:, sl].astype(bf16)]
        return _store_scores(s_ref, q, keys, bias)

    def finish(u, s_ref, m):
        qb, hp, rows, sl, start = unit(u)
        values = [(lambda a, b: v_ref[pl.ds(start + a, b - a), sl], NB_BAND),
                  (lambda a, b: vc_ref[a:b, sl].astype(bf16), PAST_LEN)]
        o, l = _softmax_values(s_ref, m, values)
        o = o / l
        o_ref[rows, sl] = jnp.where(low, o[:NB_Q], o[NB_Q:]).astype(o_ref.dtype)

    _run_units(NB_STEP_BLOCKS * n_pairs, (s0_ref, s1_ref), scores, finish)


def _attn_nb(q, k, v, kc, vc, bias, layer):
    width = B_HEADS * HEAD_DIM
    steps = NB_BLOCKS // NB_STEP_BLOCKS
    tq = NB_STEP_BLOCKS * NB_Q
    bias_specs = [
        pl.BlockSpec((B_HEADS, None, NB_Q, NB_BAND),
                     lambda b, j, qb=qb: (layer, _nb_table_index(j * NB_STEP_BLOCKS + qb), 0, 0))
        for qb in range(NB_STEP_BLOCKS)]
    return pl.pallas_call(
        _attn_nb_kernel,
        grid=(DEC_BATCH, steps),
        in_specs=[
            pl.BlockSpec((tq, width), lambda b, j: (b * steps + j, 0)),
            pl.BlockSpec((DEC_SEQ, width), lambda b, j: (b, 0)),
            pl.BlockSpec((DEC_SEQ, width), lambda b, j: (b, 0)),
            _cache_spec(layer, width),
            _cache_spec(layer, width),
        ] + bias_specs,
        out_specs=pl.BlockSpec((tq, width), lambda b, j: (b * steps + j, 0)),
        out_shape=jax.ShapeDtypeStruct((N_LAT, width), bf16),
        scratch_shapes=_score_scratch(2 * NB_Q, NB_BAND + PAST_LEN),
        compiler_params=_cparams(2),
    )(q, k, v, kc, vc, *([bias] * NB_STEP_BLOCKS))


def _nb_row_offsets():
    dr = np.full((NB_TABLES, NB_Q_ROWS, NB_BAND_ROWS), -1, np.int64)
    for t, block in enumerate(NB_TABLE_BLOCKS):
        band_start = int(np.clip(NB_Q_ROWS * block - NA_ROWS // 2, 0, GRID_ROWS - NB_BAND_ROWS))
        for qr in range(NB_Q_ROWS):
            q_row = block * NB_Q_ROWS + qr
            row_start = int(np.clip(q_row - NA_ROWS // 2, 0, GRID_ROWS - NA_ROWS))
            for kr in range(NB_BAND_ROWS):
                k_row = band_start + kr
                if row_start <= k_row < row_start + NA_ROWS:
                    dr[t, qr, kr] = int(np.clip(k_row - q_row + NA_ROWS - 1, 0, 2 * NA_ROWS - 2))
    return dr


def _nb_bias_kernel(r_ref, o_ref):
    lane = lax.broadcasted_iota(jnp.int32, (GRID_W, LANES), 1)
    q_col = lax.broadcasted_iota(jnp.int32, (GRID_W, LANES), 0)
    k_col = lane % GRID_W
    win_c = min(NA_COLS, GRID_W)
    col_start = jnp.clip(q_col - win_c // 2, 0, GRID_W - win_c)
    col_in = (k_col >= col_start) & (k_col < col_start + win_c)
    low = lane < GRID_W
    masked = jnp.full((GRID_W, LANES), MASK_VALUE, f32)
    tiles = []
    for d in range(RPB_ROWS):
        x = jnp.broadcast_to(r_ref[d:d + 1, :], (GRID_W, LANES))
        y = pltpu.roll(x, LANES - (NA_COLS - 1), 1, stride=1, stride_axis=0)
        tiles.append(jnp.where(col_in, y * LOG2E, MASK_VALUE))
    dr = _nb_row_offsets()
    for t in range(NB_TABLES):
        for qr in range(NB_Q_ROWS):
            for kp in range(NB_BAND_ROWS // 2):
                d0, d1 = int(dr[t, qr, 2 * kp]), int(dr[t, qr, 2 * kp + 1])
                left = tiles[d0] if d0 >= 0 else masked
                right = tiles[d1] if d1 >= 0 else masked
                tile = masked if (d0 < 0 and d1 < 0) else jnp.where(low, left, right)
                o_ref[t, qr * GRID_W:(qr + 1) * GRID_W, kp * LANES:(kp + 1) * LANES] = tile


def _nb_bias_tables(b_rpb):
    heads = b_rpb.shape[0] * B_HEADS
    r = b_rpb.reshape(heads, RPB_ROWS, RPB_COLS)
    half = jnp.pad(r, ((0, 0), (0, 16 - RPB_ROWS), (0, GRID_W - RPB_COLS)))
    r2 = jnp.concatenate([half, half], axis=-1)
    return pl.pallas_call(
        _nb_bias_kernel,
        grid=(heads,),
        in_specs=[pl.BlockSpec((None, 16, LANES), lambda h: (h, 0, 0))],
        out_specs=pl.BlockSpec((None, NB_TABLES, NB_Q, NB_BAND), lambda h: (h, 0, 0, 0)),
        out_shape=jax.ShapeDtypeStruct((heads, NB_TABLES, NB_Q, NB_BAND), f32),
        compiler_params=_cparams(1),
    )(r2)


def _attn_diff_kernel(*refs, tq, cache, lam_init, n_heads):
    if cache:
        lam_ref, sub_ref, q_ref, k_ref, v_ref, kc_ref, vc_ref, o_ref, s0_ref, s1_ref = refs
    else:
        lam_ref, sub_ref, q_ref, k_ref, v_ref, o_ref, s0_ref, s1_ref = refs
    low = _lane_iota() < HEAD_DIM
    seq = k_ref.shape[0]
    lp = lam_ref[...]
    lam = (jnp.exp(jnp.sum(lp[0:1] * lp[1:2], axis=-1, keepdims=True))
           - jnp.exp(jnp.sum(lp[2:3] * lp[3:4], axis=-1, keepdims=True)) + lam_init)

    def scores(h, s_ref):
        sl = slice(h * LANES, (h + 1) * LANES)
        keys = [k_ref[:, sl]]
        if cache:
            keys.append(kc_ref[:, sl].astype(bf16))
        return _store_scores(s_ref, _pair_lhs(q_ref[:, sl], low), keys)

    def finish(h, s_ref, m):
        sl = slice(h * LANES, (h + 1) * LANES)
        values = [(lambda a, b: v_ref[a:b, sl], seq)]
        if cache:
            values.append((lambda a, b: vc_ref[a:b, sl].astype(bf16), PAST_LEN))
        o, l = _softmax_values(s_ref, m, values)
        inv = 1.0 / l
        o = o[:tq] * inv[:tq] - o[tq:] * (lam * inv[tq:])
        ms = jnp.mean(o * o, axis=-1, keepdims=True)
        o = o * lax.rsqrt(ms + NORM_EPS) * sub_ref[...] * (1.0 - lam_init)
        o_ref[:, sl] = o.astype(o_ref.dtype)

    _run_units(n_heads, (s0_ref, s1_ref), scores, finish)


def _attn_diff(lam_p, subln, q, k, v, kc, vc, layer, batch, seq, tq, heads_per_step, lam_init):
    cache = kc is not None
    nq = seq // tq
    gw = heads_per_step * LANES
    in_specs = [
        pl.BlockSpec((None, 4, LANES), lambda b, g, j: (layer, 0, 0)),
        pl.BlockSpec((None, 1, LANES), lambda b, g, j: (layer, 0, 0)),
        pl.BlockSpec((tq, gw), lambda b, g, j: (b * nq + j, g)),
        pl.BlockSpec((seq, gw), lambda b, g, j: (b, g)),
        pl.BlockSpec((seq, gw), lambda b, g, j: (b, g)),
    ]
    args = [lam_p, subln, q, k, v]
    if cache:
        in_specs += [_cache_spec(layer, gw, lambda b, g, j: g)] * 2
        args += [kc, vc]
    n_keys = seq + (PAST_LEN if cache else 0)
    return pl.pallas_call(
        functools.partial(_attn_diff_kernel, tq=tq, cache=cache, lam_init=lam_init, n_heads=heads_per_step),
        grid=(batch, C_HEADS // heads_per_step, nq),
        in_specs=in_specs,
        out_specs=pl.BlockSpec((tq, gw), lambda b, g, j: (b * nq + j, g)),
        out_shape=jax.ShapeDtypeStruct((batch * seq, C_OUT), bf16),
        scratch_shapes=_score_scratch(2 * tq, n_keys),
        compiler_params=_cparams(3),
    )(*args)


def _rope_tables():
    t = jnp.arange(DEC_SEQ)
    row = (t // GRID_W).astype(f32)
    col = (t % GRID_W).astype(f32)
    n_freq = HEAD_DIM // 4
    inv_freq = ROPE_THETA ** (-jnp.arange(n_freq, dtype=f32) / n_freq)
    ang_r = row[:, None] * inv_freq
    ang_c = col[:, None] * inv_freq
    ang = jnp.concatenate([ang_r, ang_r, ang_c, ang_c], axis=-1)
    cos = jnp.cos(ang)
    sin = jnp.sin(ang)
    sign = jnp.asarray(np.tile(np.repeat(np.array([-1.0, 1.0], np.float32), n_freq), 2))
    return jnp.tile(cos, (1, 2)), jnp.tile(sin * sign, (1, 2))


def kernel(x_prompt, x_sample, cache_a_k, cache_a_v, cache_b_k, cache_b_v, cache_c_k, cache_c_v, c, c_ctx,
           w_mod, b_mod, norm_w, ffn_w1, ffn_w3, ffn_w2, w_in_ab, w_out_ab, a_q_norm, a_k_norm, b_rpb,
           w_in_c, w_out_c, c_lambda, c_subln, final_norm):
    lam_inits = [0.8 - 0.6 * math.exp(-0.3 * l) for l in range(DEPTH)]
    n_even = w_in_ab.shape[0]
    n_odd = w_in_c.shape[0]

    mod = _modulation(c, c_ctx, w_mod, b_mod)
    w1 = ffn_w1.astype(bf16)
    w3 = ffn_w3.astype(bf16)
    w2 = ffn_w2.astype(bf16)
    w_in_ab16 = w_in_ab.astype(bf16)
    w_out_ab16 = w_out_ab.astype(bf16)
    w_in_c16 = w_in_c.astype(bf16)
    w_out_c16 = w_out_c.astype(bf16)
    nw = norm_w.reshape(DEPTH, 3, 1, D_MODEL)
    qn = jnp.tile(a_q_norm, (1, 2)).reshape(n_even, 1, LANES)
    kn = jnp.tile(a_k_norm, (1, 2)).reshape(n_even, 1, LANES)
    lam_p = jnp.pad(c_lambda, ((0, 0), (0, 0), (0, LANES - HEAD_DIM)))
    subln = c_subln.reshape(n_odd, 1, 2 * HEAD_DIM)
    rope_tabs = _rope_tables()
    nb_bias = _nb_bias_tables(b_rpb)

    ca_k = cache_a_k.reshape(DEC_BATCH, n_even, PAST_LEN, A_KV_HEADS * HEAD_DIM)
    ca_v = cache_a_v.reshape(DEC_BATCH, n_even, PAST_LEN, A_KV_HEADS * HEAD_DIM)
    cb_k = cache_b_k.reshape(DEC_BATCH, n_even, PAST_LEN, B_HEADS * HEAD_DIM)
    cb_v = cache_b_v.reshape(DEC_BATCH, n_even, PAST_LEN, B_HEADS * HEAD_DIM)
    cc_k = cache_c_k.reshape(DEC_BATCH, n_odd, PAST_LEN, C_OUT)
    cc_v = cache_c_v.reshape(DEC_BATCH, n_odd, PAST_LEN, C_OUT)

    xs = x_sample.reshape(N_LAT, D_MODEL)
    xp = x_prompt.reshape(N_CTX, D_MODEL)
    fn = final_norm.reshape(1, D_MODEL)
    ffn_w = (mod, nw, w1, w3, w2)
    new_ab = None
    new_c = None

    for l in range(DEPTH):
        fin = fn if l == DEPTH - 1 else None
        if l % 2 == 0:
            e = l // 2
            xs, aq, ak, av, bq, bk, bv = _ffn_proj_even(xs, *ffn_w, w_in_ab16, qn, kn, rope_tabs, l, False)
            oa = _attn_a(aq, ak, av, ca_k, ca_v, e, DEC_BATCH, DEC_SEQ, 2 * A_UNIT_Q)
            ob = _attn_nb(bq, bk, bv, cb_k, cb_v, nb_bias, e)
            xs = _mix_ffn(xs, *ffn_w, l, DEC_SEQ, (oa, ob), w_out_ab16, e, fin)

            xp, aq, ak, av, bq, bk, bv, *new_ab = _ffn_proj_even(xp, *ffn_w, w_in_ab16, qn, kn, None, l, True,
                                                                 new_ab)
            oa = _attn_a(aq, ak, av, None, None, e, BATCH, SEQ, SEQ)
            ob = _attn_pair_ctx(bq, bk, bv, BATCH, SEQ)
            xp = _mix_ffn(xp, *ffn_w, l, None, (oa, ob), w_out_ab16, e, fin)
        else:
            o = l // 2
            xs, q, k, v = _ffn_proj_odd(xs, *ffn_w, w_in_c16, rope_tabs, l, False)
            oc = _attn_diff(lam_p, subln, q, k, v, cc_k, cc_v, o, DEC_BATCH, DEC_SEQ, 256, 4, lam_inits[l])
            xs = _mix_ffn(xs, *ffn_w, l, DEC_SEQ, (oc,), w_out_c16, o, fin)

            xp, q, k, v, *new_c = _ffn_proj_odd(xp, *ffn_w, w_in_c16, None, l, True, new_c)
            oc = _attn_diff(lam_p, subln, q, k, v, None, None, o, BATCH, SEQ, SEQ, C_HEADS, lam_inits[l])
            xp = _mix_ffn(xp, *ffn_w, l, None, (oc,), w_out_c16, o, fin)

    y_sample = xs.reshape(DEC_BATCH, DEC_SEQ, D_MODEL)
    y_prompt = xp.reshape(BATCH, SEQ, D_MODEL)
    ak, av, bk, bv = new_ab
    ck, cv = new_c
    return (y_prompt, y_sample,
            ak.reshape(BATCH, n_even, SEQ, A_KV_HEADS, HEAD_DIM), av.reshape(BATCH, n_even, SEQ, A_KV_HEADS, HEAD_DIM),
            bk.reshape(BATCH, n_even, SEQ, B_HEADS, HEAD_DIM), bv.reshape(BATCH, n_even, SEQ, B_HEADS, HEAD_DIM),
            ck.reshape(BATCH, n_odd, SEQ, C_HEADS, 2 * HEAD_DIM), cv.reshape(BATCH, n_odd, SEQ, C_HEADS, 2 * HEAD_DIM))
```

```python
import functools
import math

import numpy as np
import jax
import jax.numpy as jnp
from jax import lax
from jax.experimental import pallas as pl
from jax.experimental.pallas import tpu as pltpu

D_MODEL = 1024
BATCH = 16
SEQ = 256
DEPTH = 4
DEC_BATCH = 8
DEC_SEQ = 2048
PAST_LEN = 512
GRID_W = 64
GRID_ROWS = DEC_SEQ // GRID_W
HEAD_DIM = 64
A_HEADS = 8
A_KV_HEADS = 2
B_HEADS = 8
C_HEADS = 8
NA_ROWS = 8
NA_COLS = 16
D_FF = 2816
N_MOD = 9
ROPE_THETA = 10000.0
NORM_EPS = 1e-6
MASK_VALUE = -1e30
AB_IN = (A_HEADS + 2 * A_KV_HEADS + 3 * B_HEADS) * HEAD_DIM
AB_OUT = (A_HEADS + B_HEADS) * HEAD_DIM
C_IN = 3 * C_HEADS * 2 * HEAD_DIM
C_OUT = C_HEADS * 2 * HEAD_DIM

LANES = 128
FF_CHUNK = 256
KEY_CHUNK = 512
MOD_ROWS = 16
CTX_MOD_ROW = DEC_BATCH
VMEM_LIMIT = 52 * 1024 * 1024
TOKEN_TILE = 512

N_LAT = DEC_BATCH * DEC_SEQ
N_CTX = BATCH * SEQ

LOG2E = math.log2(math.e)
Q_SCALE = HEAD_DIM ** -0.5 * LOG2E

A_UNIT_Q = 128
A_STEP_Q = 4 * A_UNIT_Q

DIFF_UNIT_Q = 256
DIFF_STEP_Q = 2 * DIFF_UNIT_Q
DIFF_STEP_HEADS = 4

NB_Q_ROWS = 2
NB_Q = NB_Q_ROWS * GRID_W
NB_BAND_ROWS = 10
NB_BAND = NB_BAND_ROWS * GRID_W
NB_BLOCKS = DEC_SEQ // NB_Q
NB_STEP_BLOCKS = 4
NB_TABLE_BLOCKS = (0, 1, 2, NB_BLOCKS - 2, NB_BLOCKS - 1)
NB_TABLES = len(NB_TABLE_BLOCKS)
RPB_ROWS = 2 * NA_ROWS - 1
RPB_COLS = 2 * NA_COLS - 1

bf16 = jnp.bfloat16
f32 = jnp.float32


def _dot(a, b):
    return jnp.dot(a, b, preferred_element_type=f32)


def _dot_nt(a, b):
    return lax.dot_general(a, b, (((1,), (1,)), ((), ())), preferred_element_type=f32)


def _cparams(n_axes):
    return pltpu.CompilerParams(dimension_semantics=("arbitrary",) * n_axes,
                                vmem_limit_bytes=VMEM_LIMIT)


def _resident_spec(shape, *lead):
    block = (None,) * len(lead) + tuple(shape)
    index = tuple(lead) + (0,) * len(shape)
    return pl.BlockSpec(block, lambda *_: index, pipeline_mode=pl.Buffered(1))


def _modulated_norm(x, mod_ref, nw_ref):
    ms = jnp.mean(x * x, axis=-1, keepdims=True)
    y = x * lax.rsqrt(ms + NORM_EPS) * nw_ref[...]
    return y * (1.0 + mod_ref[1:2, :]) + mod_ref[0:1, :]


def _mod_kernel(c_ref, w_ref, b_ref, o_ref):
    c = c_ref[...]
    s = (c * jax.nn.sigmoid(c)).astype(bf16)
    o_ref[...] = _dot(s, w_ref[...].astype(bf16)) + b_ref[...]


def _modulation(c, c_ctx, w_mod, b_mod):
    cc = jnp.zeros((MOD_ROWS, D_MODEL), f32)
    cc = cc.at[:DEC_BATCH].set(c).at[CTX_MOD_ROW].set(c_ctx)
    out = pl.pallas_call(
        _mod_kernel,
        grid=(DEPTH, N_MOD),
        in_specs=[
            pl.BlockSpec((MOD_ROWS, D_MODEL), lambda l, k: (0, 0)),
            pl.BlockSpec((None, D_MODEL, D_MODEL), lambda l, k: (l, 0, k)),
            pl.BlockSpec((None, None, 1, D_MODEL), lambda l, k: (l, k, 0, 0)),
        ],
        out_specs=pl.BlockSpec((None, None, MOD_ROWS, D_MODEL), lambda l, k: (l, k, 0, 0)),
        out_shape=jax.ShapeDtypeStruct((DEPTH, N_MOD, MOD_ROWS, D_MODEL), f32),
        compiler_params=_cparams(2),
    )(cc, w_mod, b_mod.reshape(DEPTH, N_MOD, 1, D_MODEL))
    return out.reshape(DEPTH, 3, 3, MOD_ROWS, D_MODEL).transpose(0, 1, 3, 2, 4)


def _mod_spec(layer, sub, tiles_per_seq):
    if tiles_per_seq is None:
        return pl.BlockSpec((None, None, None, 3, D_MODEL),
                            lambda i, *_: (layer, sub, CTX_MOD_ROW, 0, 0))
    return pl.BlockSpec((None, None, None, 3, D_MODEL),
                        lambda i, *_: (layer, sub, i // tiles_per_seq, 0, 0))


N_FFN_REFS = 5


def _swiglu_step(x, mod_ref, nw_ref, w1_ref, w3_ref, w2_ref):
    h = _modulated_norm(x, mod_ref, nw_ref).astype(bf16)
    acc = None
    for c0 in range(0, D_FF, FF_CHUNK):
        a1 = _dot(h, w1_ref[:, c0:c0 + FF_CHUNK])
        a3 = _dot(h, w3_ref[:, c0:c0 + FF_CHUNK])
        g = (a1 * jax.nn.sigmoid(a1) * a3).astype(bf16)
        t = _dot(g, w2_ref[c0:c0 + FF_CHUNK, :])
        acc = t if acc is None else acc + t
    return x + 0.5 * mod_ref[2:3, :] * acc


def _ffn_specs(mod, nw, w1, w3, w2, layer, half, tiles_per_seq):
    sub = 2 * half
    specs = [
        _mod_spec(layer, sub, tiles_per_seq),
        _resident_spec((1, D_MODEL), layer, sub),
        _resident_spec((D_MODEL, D_FF), layer, half),
        _resident_spec((D_MODEL, D_FF), layer, half),
        _resident_spec((D_FF, D_MODEL), layer, half),
    ]
    return specs, [mod, nw, w1, w3, w2]


def _mix_ffn_kernel(*refs, n_parts, final):
    x_ref, mix_mod_ref = refs[:2]
    o_refs = refs[2:2 + n_parts]
    wo_refs = refs[2 + n_parts:2 + 2 * n_parts]
    pos = 2 + 2 * n_parts
    y = _dot(o_refs[0][...], wo_refs[0][...])
    for o_ref, w_ref in zip(o_refs[1:], wo_refs[1:]):
        y = y + _dot(o_ref[...], w_ref[...])
    x = x_ref[...] + mix_mod_ref[2:3, :] * y
    x = _swiglu_step(x, *refs[pos:pos + N_FFN_REFS])
    pos += N_FFN_REFS
    if final:
        fn_ref, out_ref = refs[pos:]
        ms = jnp.mean(x * x, axis=-1, keepdims=True)
        x = x * lax.rsqrt(ms + NORM_EPS) * fn_ref[...]
    else:
        out_ref, = refs[pos:]
    out_ref[...] = x


def _mix_ffn(x, mod, nw, w1, w3, w2, layer, seq_len, parts, w_out, w_lead, final_w=None):
    n = x.shape[0]
    tm = TOKEN_TILE
    tiles_per_seq = None if seq_len is None else seq_len // tm
    row = lambda i: (i, 0)
    in_specs = [pl.BlockSpec((tm, D_MODEL), row), _mod_spec(layer, 1, tiles_per_seq)]
    args = [x, mod]
    in_specs += [pl.BlockSpec((tm, p.shape[1]), row) for p in parts]
    args += list(parts)
    r0 = 0
    for p in parts:
        rows = p.shape[1]
        in_specs.append(pl.BlockSpec((None, rows, D_MODEL), lambda i, b=r0 // rows: (w_lead, b, 0),
                                     pipeline_mode=pl.Buffered(1)))
        args.append(w_out)
        r0 += rows
    ffn_specs, ffn_args = _ffn_specs(mod, nw, w1, w3, w2, layer, 1, tiles_per_seq)
    in_specs += ffn_specs
    args += ffn_args
    if final_w is not None:
        in_specs.append(_resident_spec((1, D_MODEL)))
        args.append(final_w)
    return pl.pallas_call(
        functools.partial(_mix_ffn_kernel, n_parts=len(parts), final=final_w is not None),
        grid=(n // tm,),
        in_specs=in_specs,
        out_specs=pl.BlockSpec((tm, D_MODEL), row),
        out_shape=jax.ShapeDtypeStruct((n, D_MODEL), f32),
        compiler_params=_cparams(1),
    )(*args)


def _lane_iota():
    return lax.broadcasted_iota(jnp.int32, (1, LANES), 1)


def _head_rms(z, low):
    sq = z * z
    s_low = jnp.sum(jnp.where(low, sq, 0.0), axis=-1, keepdims=True)
    s_high = jnp.sum(jnp.where(low, 0.0, sq), axis=-1, keepdims=True)
    return jnp.where(low, s_low, s_high) * (1.0 / HEAD_DIM)


def _rope(z, cos, sin_signed, first_half):
    rot = jnp.where(first_half, pltpu.roll(z, LANES - 16, 1), pltpu.roll(z, 16, 1))
    return z * cos + rot * sin_signed


def _cache_rows(z):
    return z.reshape(TOKEN_TILE // SEQ, SEQ, z.shape[1])


def _proj_even_kernel(*refs, rope, ctx, n_alias):
    x_ref = refs[0]
    pos = 1 + N_FFN_REFS
    x = _swiglu_step(x_ref[...], *refs[1:pos])
    mod_ref, nw_ref, w_ref, qn_ref, kn_ref = refs[pos:pos + 5]
    pos += 5
    if rope:
        cos_ref, sin_ref = refs[pos:pos + 2]
        pos += 2
    pos += n_alias
    xo_ref, aq_ref, ak_ref, av_ref, bq_ref, bk_ref, bv_ref = refs[pos:pos + 7]
    if ctx:
        akf_ref, avf_ref, bkf_ref, bvf_ref = refs[pos + 7:pos + 11]
    xo_ref[...] = x

    h = _modulated_norm(x, mod_ref, nw_ref).astype(bf16)
    y = _dot(h, w_ref[...])
    lane = _lane_iota()
    low = lane < HEAD_DIM
    first_half = (lane % 32) < 16
    if rope:
        cos = cos_ref[...]
        sin = sin_ref[...]

    for j in range(A_HEADS // 2):
        z = y[:, j * LANES:(j + 1) * LANES]
        z = z * lax.rsqrt(_head_rms(z, low) + NORM_EPS) * qn_ref[...]
        if rope:
            z = _rope(z, cos, sin, first_half)
        z = z * Q_SCALE
        zr = pltpu.roll(z, HEAD_DIM, 1)
        g = j // 2
        keep = low if g == 0 else jnp.logical_not(low)
        h0 = z if g == 0 else zr
        h1 = zr if g == 0 else z
        aq_ref[:, (2 * j) * LANES:(2 * j + 1) * LANES] = jnp.where(keep, h0, 0.0).astype(bf16)
        aq_ref[:, (2 * j + 1) * LANES:(2 * j + 2) * LANES] = jnp.where(keep, h1, 0.0).astype(bf16)

    o = A_HEADS * HEAD_DIM
    z = y[:, o:o + LANES]
    z = z * lax.rsqrt(_head_rms(z, low) + NORM_EPS) * kn_ref[...]
    if ctx:
        akf_ref[...] = _cache_rows(z)
    if rope:
        z = _rope(z, cos, sin, first_half)
    ak_ref[...] = z.astype(bf16)
    o += LANES
    z = y[:, o:o + LANES]
    if ctx:
        avf_ref[...] = _cache_rows(z)
    av_ref[...] = z.astype(bf16)
    o += LANES
    nb = B_HEADS * HEAD_DIM
    bq_ref[...] = (y[:, o:o + nb] * Q_SCALE).astype(bf16)
    z = y[:, o + nb:o + 2 * nb]
    if ctx:
        bkf_ref[...] = _cache_rows(z)
    bk_ref[...] = z.astype(bf16)
    z = y[:, o + 2 * nb:o + 3 * nb]
    if ctx:
        bvf_ref[...] = _cache_rows(z)
    bv_ref[...] = z.astype(bf16)


def _new_cache_outputs(widths, n_layers, slot, prev):
    seqs = TOKEN_TILE // SEQ
    shapes = [jax.ShapeDtypeStruct((BATCH, n_layers, SEQ, w), f32) for w in widths]
    specs = [pl.BlockSpec((seqs, None, SEQ, w), lambda i: (i, slot, 0, 0)) for w in widths]
    alias_specs = [pl.BlockSpec(memory_space=pl.ANY)] * (len(widths) if prev is not None else 0)
    return shapes, specs, alias_specs


def _ffn_proj_even(x, mod, nw, w1, w3, w2, w_in, qn, kn, rope_tabs, layer, ctx, prev_caches=None):
    n = x.shape[0]
    tm = TOKEN_TILE
    e = layer // 2
    rope = rope_tabs is not None
    tiles_per_seq = None if ctx else DEC_SEQ // tm
    row = lambda i: (i, 0)
    ffn_specs, ffn_args = _ffn_specs(mod, nw, w1, w3, w2, layer, 0, tiles_per_seq)
    in_specs = [pl.BlockSpec((tm, D_MODEL), row)] + ffn_specs + [
        _mod_spec(layer, 1, tiles_per_seq),
        _resident_spec((1, D_MODEL), layer, 1),
        _resident_spec((D_MODEL, AB_IN), e),
        _resident_spec((1, LANES), e),
        _resident_spec((1, LANES), e),
    ]
    args = [x] + ffn_args + [mod, nw, w_in, qn, kn]
    if rope:
        pos_spec = pl.BlockSpec((tm, LANES), lambda i: (i % tiles_per_seq, 0))
        in_specs += [pos_spec, pos_spec]
        args += list(rope_tabs)
    widths = [2 * A_HEADS * HEAD_DIM, LANES, LANES, 512, 512, 512]
    out_shape = [jax.ShapeDtypeStruct((n, D_MODEL), f32)] + [jax.ShapeDtypeStruct((n, w), bf16) for w in widths]
    out_specs = [pl.BlockSpec((tm, D_MODEL), row)] + [pl.BlockSpec((tm, w), row) for w in widths]
    aliases = {}
    n_alias = 0
    if ctx:
        shapes, specs, alias_specs = _new_cache_outputs((LANES, LANES, 512, 512), w_in.shape[0], e, prev_caches)
        n_alias = len(alias_specs)
        aliases = {len(args) + k: len(out_shape) + k for k in range(n_alias)}
        in_specs += alias_specs
        args += list(prev_caches or ())
        out_shape += shapes
        out_specs += specs
    return pl.pallas_call(
        functools.partial(_proj_even_kernel, rope=rope, ctx=ctx, n_alias=n_alias),
        grid=(n // tm,),
        in_specs=in_specs,
        out_specs=out_specs,
        out_shape=out_shape,
        input_output_aliases=aliases,
        compiler_params=_cparams(1),
    )(*args)


def _proj_odd_kernel(*refs, rope, ctx, n_alias):
    x_ref = refs[0]
    pos = 1 + N_FFN_REFS
    x = _swiglu_step(x_ref[...], *refs[1:pos])
    mod_ref, nw_ref, w_ref = refs[pos:pos + 3]
    pos += 3
    if rope:
        cos_ref, sin_ref = refs[pos:pos + 2]
        pos += 2
    pos += n_alias
    xo_ref, q_ref, k_ref, v_ref = refs[pos:pos + 4]
    if ctx:
        kf_ref, vf_ref = refs[pos + 4:pos + 6]
    xo_ref[...] = x

    h = _modulated_norm(x, mod_ref, nw_ref).astype(bf16)
    y = _dot(h, w_ref[...])
    lane = _lane_iota()
    first_half = (lane % 32) < 16
    for j in range(C_OUT // LANES):
        zq = y[:, j * LANES:(j + 1) * LANES]
        zk = y[:, C_OUT + j * LANES:C_OUT + (j + 1) * LANES]
        if ctx:
            kf_ref[:, :, j * LANES:(j + 1) * LANES] = _cache_rows(zk)
        if rope:
            zq = _rope(zq, cos_ref[...], sin_ref[...], first_half)
            zk = _rope(zk, cos_ref[...], sin_ref[...], first_half)
        q_ref[:, j * LANES:(j + 1) * LANES] = (zq * Q_SCALE).astype(bf16)
        k_ref[:, j * LANES:(j + 1) * LANES] = zk.astype(bf16)
    zv = y[:, 2 * C_OUT:3 * C_OUT]
    if ctx:
        vf_ref[...] = _cache_rows(zv)
    v_ref[...] = zv.astype(bf16)


def _ffn_proj_odd(x, mod, nw, w1, w3, w2, w_in, rope_tabs, layer, ctx, prev_caches=None):
    n = x.shape[0]
    tm = TOKEN_TILE
    o = layer // 2
    rope = rope_tabs is not None
    tiles_per_seq = None if ctx else DEC_SEQ // tm
    row = lambda i: (i, 0)
    ffn_specs, ffn_args = _ffn_specs(mod, nw, w1, w3, w2, layer, 0, tiles_per_seq)
    in_specs = [pl.BlockSpec((tm, D_MODEL), row)] + ffn_specs + [
        _mod_spec(layer, 1, tiles_per_seq),
        _resident_spec((1, D_MODEL), layer, 1),
        _resident_spec((D_MODEL, C_IN), o),
    ]
    args = [x] + ffn_args + [mod, nw, w_in]
    if rope:
        pos_spec = pl.BlockSpec((tm, LANES), lambda i: (i % tiles_per_seq, 0))
        in_specs += [pos_spec, pos_spec]
        args += list(rope_tabs)
    out_shape = [jax.ShapeDtypeStruct((n, D_MODEL), f32)] + [jax.ShapeDtypeStruct((n, C_OUT), bf16)] * 3
    out_specs = [pl.BlockSpec((tm, D_MODEL), row)] + [pl.BlockSpec((tm, C_OUT), row)] * 3
    aliases = {}
    n_alias = 0
    if ctx:
        shapes, specs, alias_specs = _new_cache_outputs((C_OUT, C_OUT), w_in.shape[0], o, prev_caches)
        n_alias = len(alias_specs)
        aliases = {len(args) + k: len(out_shape) + k for k in range(n_alias)}
        in_specs += alias_specs
        args += list(prev_caches or ())
        out_shape += shapes
        out_specs += specs
    return pl.pallas_call(
        functools.partial(_proj_odd_kernel, rope=rope, ctx=ctx, n_alias=n_alias),
        grid=(n // tm,),
        in_specs=in_specs,
        out_specs=out_specs,
        out_shape=out_shape,
        input_output_aliases=aliases,
        compiler_params=_cparams(1),
    )(*args)


def _lane_tile_reduce(x, op):
    acc = x[:, :LANES]
    for i in range(1, x.shape[1] // LANES):
        acc = op(acc, x[:, i * LANES:(i + 1) * LANES])
    return acc


def _store_scores(s_ref, q, keys, bias=None):
    off = 0
    for i, k in enumerate(keys):
        s = _dot_nt(q, k)
        if i == 0 and bias is not None:
            s = s + bias
        s_ref[:, off:off + k.shape[0]] = s
        off += k.shape[0]
    return _lane_tile_reduce(s_ref[...], jnp.maximum).max(axis=-1, keepdims=True)


def _softmax_values(s_ref, m, values):
    l_acc = None
    o = None
    off = 0
    for load, n in values:
        for c0 in range(0, n, KEY_CHUNK):
            c1 = min(c0 + KEY_CHUNK, n)
            p = jnp.exp2(s_ref[:, off + c0:off + c1] - m)
            lt = _lane_tile_reduce(p, jnp.add)
            l_acc = lt if l_acc is None else l_acc + lt
            t = _dot(p.astype(bf16), load(c0, c1))
            o = t if o is None else o + t
        off += n
    return o, l_acc.sum(axis=-1, keepdims=True)


def _run_units(n_units, s_refs, scores, finish):
    m_next = scores(0, s_refs[0])
    for u in range(n_units):
        m = m_next
        if u + 1 < n_units:
            m_next = scores(u + 1, s_refs[(u + 1) % 2])
        finish(u, s_refs[u % 2], m)


def _score_scratch(rows, n_keys):
    return [pltpu.VMEM((rows, n_keys), f32)] * 2


def _cache_spec(layer, width, col_block=lambda *idx: 0):
    return pl.BlockSpec((None, None, PAST_LEN, width), lambda b, *rest: (b, layer, 0, col_block(b, *rest)))


def _attn_a_kernel(*refs, tq, cache):
    if cache:
        q_ref, k_ref, v_ref, kc_ref, vc_ref, o_ref, s0_ref, s1_ref = refs
    else:
        q_ref, k_ref, v_ref, o_ref, s0_ref, s1_ref = refs
    low = _lane_iota() < HEAD_DIM
    group = A_HEADS // A_KV_HEADS
    seq = k_ref.shape[0]
    n_blocks = tq // A_UNIT_Q

    def unit(u):
        g, qb = divmod(u, n_blocks)
        return g, slice(qb * A_UNIT_Q, (qb + 1) * A_UNIT_Q)

    def scores(u, s_ref):
        g, rows = unit(u)
        q = jnp.concatenate(
            [q_ref[rows, (group * g + i) * LANES:(group * g + i + 1) * LANES] for i in range(group)], axis=0)
        keys = [k_ref[...]]
        if cache:
            keys.append(kc_ref[...].astype(bf16))
        return _store_scores(s_ref, q, keys)

    def finish(u, s_ref, m):
        g, rows = unit(u)
        values = [(lambda a, b: v_ref[a:b, :], seq)]
        if cache:
            values.append((lambda a, b: vc_ref[a:b, :].astype(bf16), PAST_LEN))
        o, l = _softmax_values(s_ref, m, values)
        o = o / l
        o_other = pltpu.roll(o, HEAD_DIM, 1)
        o_low, o_high = (o, o_other) if g == 0 else (o_other, o)
        for jj in range(group // 2):
            even = o_low[(2 * jj) * A_UNIT_Q:(2 * jj + 1) * A_UNIT_Q]
            odd = o_high[(2 * jj + 1) * A_UNIT_Q:(2 * jj + 2) * A_UNIT_Q]
            blk = (group // 2) * g + jj
            o_ref[rows, blk * LANES:(blk + 1) * LANES] = jnp.where(low, even, odd).astype(o_ref.dtype)

    _run_units(A_KV_HEADS * n_blocks, (s0_ref, s1_ref), scores, finish)


def _attn_a(q, k, v, kc, vc, layer, batch, seq, tq):
    cache = kc is not None
    nq = seq // tq
    in_specs = [
        pl.BlockSpec((tq, 2 * A_HEADS * HEAD_DIM), lambda b, j: (b * nq + j, 0)),
        pl.BlockSpec((seq, LANES), lambda b, j: (b, 0)),
        pl.BlockSpec((seq, LANES), lambda b, j: (b, 0)),
    ]
    args = [q, k, v]
    if cache:
        in_specs += [_cache_spec(layer, LANES)] * 2
        args += [kc, vc]
    n_keys = seq + (PAST_LEN if cache else 0)
    return pl.pallas_call(
        functools.partial(_attn_a_kernel, tq=tq, cache=cache),
        grid=(batch, nq),
        in_specs=in_specs,
        out_specs=pl.BlockSpec((tq, A_HEADS * HEAD_DIM), lambda b, j: (b * nq + j, 0)),
        out_shape=jax.ShapeDtypeStruct((batch * seq, A_HEADS * HEAD_DIM), bf16),
        scratch_shapes=_score_scratch(A_HEADS // A_KV_HEADS * A_UNIT_Q, n_keys),
        compiler_params=_cparams(2),
    )(*args)


def _pair_lhs(q, low):
    zero = jnp.zeros_like(q)
    return jnp.concatenate([jnp.where(low, q, zero), jnp.where(low, zero, q)], axis=0)


def _attn_pair_ctx_kernel(q_ref, k_ref, v_ref, o_ref, s0_ref, s1_ref, *, tq):
    low = _lane_iota() < HEAD_DIM
    seq = k_ref.shape[0]

    def scores(hp, s_ref):
        sl = slice(hp * LANES, (hp + 1) * LANES)
        return _store_scores(s_ref, _pair_lhs(q_ref[:, sl], low), [k_ref[:, sl]])

    def finish(hp, s_ref, m):
        sl = slice(hp * LANES, (hp + 1) * LANES)
        o, l = _softmax_values(s_ref, m, [(lambda a, b: v_ref[a:b, sl], seq)])
        o = o / l
        o_ref[:, sl] = jnp.where(low, o[:tq], o[tq:]).astype(o_ref.dtype)

    _run_units(B_HEADS // 2, (s0_ref, s1_ref), scores, finish)


def _attn_pair_ctx(q, k, v, batch, seq):
    spec = pl.BlockSpec((seq, B_HEADS * HEAD_DIM), lambda b: (b, 0))
    return pl.pallas_call(
        functools.partial(_attn_pair_ctx_kernel, tq=seq),
        grid=(batch,),
        in_specs=[spec, spec, spec],
        out_specs=spec,
        out_shape=jax.ShapeDtypeStruct((batch * seq, B_HEADS * HEAD_DIM), bf16),
        scratch_shapes=_score_scratch(2 * seq, seq),
        compiler_params=_cparams(1),
    )(q, k, v)


def _nb_band_start(block):
    return jnp.clip(NB_Q_ROWS * block - NA_ROWS // 2, 0, GRID_ROWS - NB_BAND_ROWS)


def _nb_table_index(block):
    return jnp.where(block < 2, block,
                     jnp.where(block >= NB_BLOCKS - 2, block - (NB_BLOCKS - NB_TABLES), 2))


def _attn_nb_kernel(*refs):
    q_ref, k_ref, v_ref, kc_ref, vc_ref = refs[:5]
    bias_refs = refs[5:5 + NB_STEP_BLOCKS]
    o_ref, s0_ref, s1_ref = refs[5 + NB_STEP_BLOCKS:]
    j = pl.program_id(1)
    low = _lane_iota() < HEAD_DIM
    n_pairs = B_HEADS // 2

    def unit(u):
        qb, hp = divmod(u, n_pairs)
        start = pl.multiple_of(_nb_band_start(j * NB_STEP_BLOCKS + qb) * GRID_W, GRID_W)
        return qb, hp, slice(qb * NB_Q, (qb + 1) * NB_Q), slice(hp * LANES, (hp + 1) * LANES), start

    def scores(u, s_ref):
        qb, hp, rows, sl, start = unit(u)
        q = _pair_lhs(q_ref[rows, sl], low)
        bias = jnp.concatenate([bias_refs[qb][2 * hp], bias_refs[qb][2 * hp + 1]], axis=0)
        keys = [k_ref[pl.ds(start, NB_BAND), sl], kc_ref[:, sl].astype(bf16)]
        return _store_scores(s_ref, q, keys, bias)

    def finish(u, s_ref, m):
        qb, hp, rows, sl, start = unit(u)
        values = [(lambda a, b: v_ref[pl.ds(start + a, b - a), sl], NB_BAND),
                  (lambda a, b: vc_ref[a:b, sl].astype(bf16), PAST_LEN)]
        o, l = _softmax_values(s_ref, m, values)
        o = o / l
        o_ref[rows, sl] = jnp.where(low, o[:NB_Q], o[NB_Q:]).astype(o_ref.dtype)

    _run_units(NB_STEP_BLOCKS * n_pairs, (s0_ref, s1_ref), scores, finish)


def _attn_nb(q, k, v, kc, vc, bias, layer):
    width = B_HEADS * HEAD_DIM
    steps = NB_BLOCKS // NB_STEP_BLOCKS
    tq = NB_STEP_BLOCKS * NB_Q
    bias_specs = [
        pl.BlockSpec((B_HEADS, None, NB_Q, NB_BAND),
                     lambda b, j, qb=qb: (layer, _nb_table_index(j * NB_STEP_BLOCKS + qb), 0, 0))
        for qb in range(NB_STEP_BLOCKS)]
    return pl.pallas_call(
        _attn_nb_kernel,
        grid=(DEC_BATCH, steps),
        in_specs=[
            pl.BlockSpec((tq, width), lambda b, j: (b * steps + j, 0)),
            pl.BlockSpec((DEC_SEQ, width), lambda b, j: (b, 0)),
            pl.BlockSpec((DEC_SEQ, width), lambda b, j: (b, 0)),
            _cache_spec(layer, width),
            _cache_spec(layer, width),
        ] + bias_specs,
        out_specs=pl.BlockSpec((tq, width), lambda b, j: (b * steps + j, 0)),
        out_shape=jax.ShapeDtypeStruct((N_LAT, width), bf16),
        scratch_shapes=_score_scratch(2 * NB_Q, NB_BAND + PAST_LEN),
        compiler_params=_cparams(2),
    )(q, k, v, kc, vc, *([bias] * NB_STEP_BLOCKS))


def _nb_row_offsets():
    dr = np.full((NB_TABLES, NB_Q_ROWS, NB_BAND_ROWS), -1, np.int64)
    for t, block in enumerate(NB_TABLE_BLOCKS):
        band_start = int(np.clip(NB_Q_ROWS * block - NA_ROWS // 2, 0, GRID_ROWS - NB_BAND_ROWS))
        for qr in range(NB_Q_ROWS):
            q_row = block * NB_Q_ROWS + qr
            row_start = int(np.clip(q_row - NA_ROWS // 2, 0, GRID_ROWS - NA_ROWS))
            for kr in range(NB_BAND_ROWS):
                k_row = band_start + kr
                if row_start <= k_row < row_start + NA_ROWS:
                    dr[t, qr, kr] = int(np.clip(k_row - q_row + NA_ROWS - 1, 0, 2 * NA_ROWS - 2))
    return dr


def _nb_bias_kernel(r_ref, o_ref):
    lane = lax.broadcasted_iota(jnp.int32, (GRID_W, LANES), 1)
    q_col = lax.broadcasted_iota(jnp.int32, (GRID_W, LANES), 0)
    k_col = lane % GRID_W
    win_c = min(NA_COLS, GRID_W)
    col_start = jnp.clip(q_col - win_c // 2, 0, GRID_W - win_c)
    col_in = (k_col >= col_start) & (k_col < col_start + win_c)
    low = lane < GRID_W
    masked = jnp.full((GRID_W, LANES), MASK_VALUE, f32)
    tiles = []
    for d in range(RPB_ROWS):
        x = jnp.broadcast_to(r_ref[d:d + 1, :], (GRID_W, LANES))
        y = pltpu.roll(x, LANES - (NA_COLS - 1), 1, stride=1, stride_axis=0)
        tiles.append(jnp.where(col_in, y * LOG2E, MASK_VALUE))
    dr = _nb_row_offsets()
    for t in range(NB_TABLES):
        for qr in range(NB_Q_ROWS):
            for kp in range(NB_BAND_ROWS // 2):
                d0, d1 = int(dr[t, qr, 2 * kp]), int(dr[t, qr, 2 * kp + 1])
                left = tiles[d0] if d0 >= 0 else masked
                right = tiles[d1] if d1 >= 0 else masked
                tile = masked if (d0 < 0 and d1 < 0) else jnp.where(low, left, right)
                o_ref[t, qr * GRID_W:(qr + 1) * GRID_W, kp * LANES:(kp + 1) * LANES] = tile


def _nb_bias_tables(b_rpb):
    heads = b_rpb.shape[0] * B_HEADS
    r = b_rpb.reshape(heads, RPB_ROWS, RPB_COLS)
    half = jnp.pad(r, ((0, 0), (0, 16 - RPB_ROWS), (0, GRID_W - RPB_COLS)))
    r2 = jnp.concatenate([half, half], axis=-1)
    return pl.pallas_call(
        _nb_bias_kernel,
        grid=(heads,),
        in_specs=[pl.BlockSpec((None, 16, LANES), lambda h: (h, 0, 0))],
        out_specs=pl.BlockSpec((None, NB_TABLES, NB_Q, NB_BAND), lambda h: (h, 0, 0, 0)),
        out_shape=jax.ShapeDtypeStruct((heads, NB_TABLES, NB_Q, NB_BAND), f32),
        compiler_params=_cparams(1),
    )(r2)


def _attn_diff_kernel(*refs, tq, cache, lam_init, n_heads):
    if cache:
        lam_ref, sub_ref, q_ref, k_ref, v_ref, kc_ref, vc_ref, o_ref, s0_ref, s1_ref = refs
    else:
        lam_ref, sub_ref, q_ref, k_ref, v_ref, o_ref, s0_ref, s1_ref = refs
    low = _lane_iota() < HEAD_DIM
    seq = k_ref.shape[0]
    lp = lam_ref[...]
    lam = (jnp.exp(jnp.sum(lp[0:1] * lp[1:2], axis=-1, keepdims=True))
           - jnp.exp(jnp.sum(lp[2:3] * lp[3:4], axis=-1, keepdims=True)) + lam_init)

    uq = DIFF_UNIT_Q
    n_blocks = tq // uq

    def unit(u):
        h, qb = divmod(u, n_blocks)
        return slice(qb * uq, (qb + 1) * uq), slice(h * LANES, (h + 1) * LANES)

    def scores(u, s_ref):
        rows, sl = unit(u)
        keys = [k_ref[:, sl]]
        if cache:
            keys.append(kc_ref[:, sl].astype(bf16))
        return _store_scores(s_ref, _pair_lhs(q_ref[rows, sl], low), keys)

    def finish(u, s_ref, m):
        rows, sl = unit(u)
        values = [(lambda a, b: v_ref[a:b, sl], seq)]
        if cache:
            values.append((lambda a, b: vc_ref[a:b, sl].astype(bf16), PAST_LEN))
        o, l = _softmax_values(s_ref, m, values)
        inv = 1.0 / l
        o = o[:uq] * inv[:uq] - o[uq:] * (lam * inv[uq:])
        ms = jnp.mean(o * o, axis=-1, keepdims=True)
        o = o * lax.rsqrt(ms + NORM_EPS) * sub_ref[...] * (1.0 - lam_init)
        o_ref[rows, sl] = o.astype(o_ref.dtype)

    _run_units(n_heads * n_blocks, (s0_ref, s1_ref), scores, finish)


def _attn_diff(lam_p, subln, q, k, v, kc, vc, layer, batch, seq, tq, heads_per_step, lam_init):
    cache = kc is not None
    nq = seq // tq
    gw = heads_per_step * LANES
    in_specs = [
        pl.BlockSpec((None, 4, LANES), lambda b, g, j: (layer, 0, 0)),
        pl.BlockSpec((None, 1, LANES), lambda b, g, j: (layer, 0, 0)),
        pl.BlockSpec((tq, gw), lambda b, g, j: (b * nq + j, g)),
        pl.BlockSpec((seq, gw), lambda b, g, j: (b, g)),
        pl.BlockSpec((seq, gw), lambda b, g, j: (b, g)),
    ]
    args = [lam_p, subln, q, k, v]
    if cache:
        in_specs += [_cache_spec(layer, gw, lambda b, g, j: g)] * 2
        args += [kc, vc]
    n_keys = seq + (PAST_LEN if cache else 0)
    return pl.pallas_call(
        functools.partial(_attn_diff_kernel, tq=tq, cache=cache, lam_init=lam_init, n_heads=heads_per_step),
        grid=(batch, C_HEADS // heads_per_step, nq),
        in_specs=in_specs,
        out_specs=pl.BlockSpec((tq, gw), lambda b, g, j: (b * nq + j, g)),
        out_shape=jax.ShapeDtypeStruct((batch * seq, C_OUT), bf16),
        scratch_shapes=_score_scratch(2 * DIFF_UNIT_Q, n_keys),
        compiler_params=_cparams(3),
    )(*args)


def _rope_tables():
    t = jnp.arange(DEC_SEQ)
    row = (t // GRID_W).astype(f32)
    col = (t % GRID_W).astype(f32)
    n_freq = HEAD_DIM // 4
    inv_freq = ROPE_THETA ** (-jnp.arange(n_freq, dtype=f32) / n_freq)
    ang_r = row[:, None] * inv_freq
    ang_c = col[:, None] * inv_freq
    ang = jnp.concatenate([ang_r, ang_r, ang_c, ang_c], axis=-1)
    cos = jnp.cos(ang)
    sin = jnp.sin(ang)
    sign = jnp.asarray(np.tile(np.repeat(np.array([-1.0, 1.0], np.float32), n_freq), 2))
    return jnp.tile(cos, (1, 2)), jnp.tile(sin * sign, (1, 2))


def kernel(x_prompt, x_sample, cache_a_k, cache_a_v, cache_b_k, cache_b_v, cache_c_k, cache_c_v, c, c_ctx,
           w_mod, b_mod, norm_w, ffn_w1, ffn_w3, ffn_w2, w_in_ab, w_out_ab, a_q_norm, a_k_norm, b_rpb,
           w_in_c, w_out_c, c_lambda, c_subln, final_norm):
    lam_inits = [0.8 - 0.6 * math.exp(-0.3 * l) for l in range(DEPTH)]
    n_even = w_in_ab.shape[0]
    n_odd = w_in_c.shape[0]

    mod = _modulation(c, c_ctx, w_mod, b_mod)
    w1 = ffn_w1.astype(bf16)
    w3 = ffn_w3.astype(bf16)
    w2 = ffn_w2.astype(bf16)
    w_in_ab16 = w_in_ab.astype(bf16)
    w_out_ab16 = w_out_ab.astype(bf16)
    w_in_c16 = w_in_c.astype(bf16)
    w_out_c16 = w_out_c.astype(bf16)
    nw = norm_w.reshape(DEPTH, 3, 1, D_MODEL)
    qn = jnp.tile(a_q_norm, (1, 2)).reshape(n_even, 1, LANES)
    kn = jnp.tile(a_k_norm, (1, 2)).reshape(n_even, 1, LANES)
    lam_p = jnp.pad(c_lambda, ((0, 0), (0, 0), (0, LANES - HEAD_DIM)))
    subln = c_subln.reshape(n_odd, 1, 2 * HEAD_DIM)
    rope_tabs = _rope_tables()
    nb_bias = _nb_bias_tables(b_rpb)

    ca_k = cache_a_k.reshape(DEC_BATCH, n_even, PAST_LEN, A_KV_HEADS * HEAD_DIM)
    ca_v = cache_a_v.reshape(DEC_BATCH, n_even, PAST_LEN, A_KV_HEADS * HEAD_DIM)
    cb_k = cache_b_k.reshape(DEC_BATCH, n_even, PAST_LEN, B_HEADS * HEAD_DIM)
    cb_v = cache_b_v.reshape(DEC_BATCH, n_even, PAST_LEN, B_HEADS * HEAD_DIM)
    cc_k = cache_c_k.reshape(DEC_BATCH, n_odd, PAST_LEN, C_OUT)
    cc_v = cache_c_v.reshape(DEC_BATCH, n_odd, PAST_LEN, C_OUT)

    xs = x_sample.reshape(N_LAT, D_MODEL)
    xp = x_prompt.reshape(N_CTX, D_MODEL)
    fn = final_norm.reshape(1, D_MODEL)
    ffn_w = (mod, nw, w1, w3, w2)
    new_ab = None
    new_c = None

    for l in range(DEPTH):
        fin = fn if l == DEPTH - 1 else None
        if l % 2 == 0:
            e = l // 2
            xs, aq, ak, av, bq, bk, bv = _ffn_proj_even(xs, *ffn_w, w_in_ab16, qn, kn, rope_tabs, l, False)
            oa = _attn_a(aq, ak, av, ca_k, ca_v, e, DEC_BATCH, DEC_SEQ, A_STEP_Q)
            ob = _attn_nb(bq, bk, bv, cb_k, cb_v, nb_bias, e)
            xs = _mix_ffn(xs, *ffn_w, l, DEC_SEQ, (oa, ob), w_out_ab16, e, fin)

            xp, aq, ak, av, bq, bk, bv, *new_ab = _ffn_proj_even(xp, *ffn_w, w_in_ab16, qn, kn, None, l, True,
                                                                 new_ab)
            oa = _attn_a(aq, ak, av, None, None, e, BATCH, SEQ, SEQ)
            ob = _attn_pair_ctx(bq, bk, bv, BATCH, SEQ)
            xp = _mix_ffn(xp, *ffn_w, l, None, (oa, ob), w_out_ab16, e, fin)
        else:
            o = l // 2
            xs, q, k, v = _ffn_proj_odd(xs, *ffn_w, w_in_c16, rope_tabs, l, False)
            oc = _attn_diff(lam_p, subln, q, k, v, cc_k, cc_v, o, DEC_BATCH, DEC_SEQ, DIFF_STEP_Q,
                            DIFF_STEP_HEADS, lam_inits[l])
            xs = _mix_ffn(xs, *ffn_w, l, DEC_SEQ, (oc,), w_out_c16, o, fin)

            xp, q, k, v, *new_c = _ffn_proj_odd(xp, *ffn_w, w_in_c16, None, l, True, new_c)
            oc = _attn_diff(lam_p, subln, q, k, v, None, None, o, BATCH, SEQ, SEQ, C_HEADS, lam_inits[l])
            xp = _mix_ffn(xp, *ffn_w, l, None, (oc,), w_out_c16, o, fin)

    y_sample = xs.reshape(DEC_BATCH, DEC_SEQ, D_MODEL)
    y_prompt = xp.reshape(BATCH, SEQ, D_MODEL)
    ak, av, bk, bv = new_ab
    ck, cv = new_c
    return (y_prompt, y_sample,
            ak.reshape(BATCH, n_even, SEQ, A_KV_HEADS, HEAD_DIM), av.reshape(BATCH, n_even, SEQ, A_KV_HEADS, HEAD_DIM),
            bk.reshape(BATCH, n_even, SEQ, B_HEADS, HEAD_DIM), bv.reshape(BATCH, n_even, SEQ, B_HEADS, HEAD_DIM),
            ck.reshape(BATCH, n_odd, SEQ, C_HEADS, 2 * HEAD_DIM), cv.reshape(BATCH, n_odd, SEQ, C_HEADS, 2 * HEAD_DIM))
```

```python
import functools
import math

import numpy as np
import jax
import jax.numpy as jnp
from jax import lax
from jax.experimental import pallas as pl
from jax.experimental.pallas import tpu as pltpu

D_MODEL = 1024
BATCH = 16
SEQ = 256
DEPTH = 4
DEC_BATCH = 8
DEC_SEQ = 2048
PAST_LEN = 512
GRID_W = 64
GRID_ROWS = DEC_SEQ // GRID_W
HEAD_DIM = 64
A_HEADS = 8
A_KV_HEADS = 2
B_HEADS = 8
C_HEADS = 8
NA_ROWS = 8
NA_COLS = 16
D_FF = 2816
N_MOD = 9
ROPE_THETA = 10000.0
NORM_EPS = 1e-6
MASK_VALUE = -1e30
AB_IN = (A_HEADS + 2 * A_KV_HEADS + 3 * B_HEADS) * HEAD_DIM
AB_OUT = (A_HEADS + B_HEADS) * HEAD_DIM
C_IN = 3 * C_HEADS * 2 * HEAD_DIM
C_OUT = C_HEADS * 2 * HEAD_DIM

LANES = 128
FF_CHUNK = 256
KEY_CHUNK = 512
MOD_ROWS = 16
CTX_MOD_ROW = DEC_BATCH
VMEM_LIMIT = 52 * 1024 * 1024
TOKEN_TILE = 512
MIX_TOKEN_TILE = 1024

N_LAT = DEC_BATCH * DEC_SEQ
N_CTX = BATCH * SEQ

LOG2E = math.log2(math.e)
Q_SCALE = HEAD_DIM ** -0.5 * LOG2E

A_UNIT_Q = 128
A_STEP_Q = 4 * A_UNIT_Q

DIFF_UNIT_Q = 256
DIFF_STEP_Q = 2 * DIFF_UNIT_Q
DIFF_STEP_HEADS = 4

NB_Q_ROWS = 2
NB_Q = NB_Q_ROWS * GRID_W
NB_BAND_ROWS = 10
NB_BAND = NB_BAND_ROWS * GRID_W
NB_BLOCKS = DEC_SEQ // NB_Q
NB_STEP_BLOCKS = 4
NB_TABLE_BLOCKS = (0, 1, 2, NB_BLOCKS - 2, NB_BLOCKS - 1)
NB_TABLES = len(NB_TABLE_BLOCKS)
RPB_ROWS = 2 * NA_ROWS - 1
RPB_COLS = 2 * NA_COLS - 1

bf16 = jnp.bfloat16
f32 = jnp.float32


def _dot(a, b):
    return jnp.dot(a, b, preferred_element_type=f32)


def _dot_nt(a, b):
    return lax.dot_general(a, b, (((1,), (1,)), ((), ())), preferred_element_type=f32)


def _cparams(n_axes):
    return pltpu.CompilerParams(dimension_semantics=("arbitrary",) * n_axes,
                                vmem_limit_bytes=VMEM_LIMIT)


def _resident_spec(shape, *lead):
    block = (None,) * len(lead) + tuple(shape)
    index = tuple(lead) + (0,) * len(shape)
    return pl.BlockSpec(block, lambda *_: index, pipeline_mode=pl.Buffered(1))


def _modulated_norm(x, mod_ref, nw_ref):
    ms = jnp.mean(x * x, axis=-1, keepdims=True)
    y = x * lax.rsqrt(ms + NORM_EPS) * nw_ref[...]
    return y * (1.0 + mod_ref[1:2, :]) + mod_ref[0:1, :]


def _mod_kernel(c_ref, w_ref, b_ref, o_ref):
    c = c_ref[...]
    s = (c * jax.nn.sigmoid(c)).astype(bf16)
    o_ref[...] = _dot(s, w_ref[...].astype(bf16)) + b_ref[...]


def _modulation(c, c_ctx, w_mod, b_mod):
    cc = jnp.zeros((MOD_ROWS, D_MODEL), f32)
    cc = cc.at[:DEC_BATCH].set(c).at[CTX_MOD_ROW].set(c_ctx)
    out = pl.pallas_call(
        _mod_kernel,
        grid=(DEPTH, N_MOD),
        in_specs=[
            pl.BlockSpec((MOD_ROWS, D_MODEL), lambda l, k: (0, 0)),
            pl.BlockSpec((None, D_MODEL, D_MODEL), lambda l, k: (l, 0, k)),
            pl.BlockSpec((None, None, 1, D_MODEL), lambda l, k: (l, k, 0, 0)),
        ],
        out_specs=pl.BlockSpec((None, None, MOD_ROWS, D_MODEL), lambda l, k: (l, k, 0, 0)),
        out_shape=jax.ShapeDtypeStruct((DEPTH, N_MOD, MOD_ROWS, D_MODEL), f32),
        compiler_params=_cparams(2),
    )(cc, w_mod, b_mod.reshape(DEPTH, N_MOD, 1, D_MODEL))
    return out.reshape(DEPTH, 3, 3, MOD_ROWS, D_MODEL).transpose(0, 1, 3, 2, 4)


def _mod_spec(layer, sub, tiles_per_seq):
    if tiles_per_seq is None:
        return pl.BlockSpec((None, None, None, 3, D_MODEL),
                            lambda i, *_: (layer, sub, CTX_MOD_ROW, 0, 0))
    return pl.BlockSpec((None, None, None, 3, D_MODEL),
                        lambda i, *_: (layer, sub, i // tiles_per_seq, 0, 0))


N_FFN_REFS = 5


def _swiglu_step(x, mod_ref, nw_ref, w1_ref, w3_ref, w2_ref):
    h = _modulated_norm(x, mod_ref, nw_ref).astype(bf16)
    acc = None
    for c0 in range(0, D_FF, FF_CHUNK):
        a1 = _dot(h, w1_ref[:, c0:c0 + FF_CHUNK])
        a3 = _dot(h, w3_ref[:, c0:c0 + FF_CHUNK])
        g = (a1 * jax.nn.sigmoid(a1) * a3).astype(bf16)
        t = _dot(g, w2_ref[c0:c0 + FF_CHUNK, :])
        acc = t if acc is None else acc + t
    return x + 0.5 * mod_ref[2:3, :] * acc


def _ffn_specs(mod, nw, w1, w3, w2, layer, half, tiles_per_seq):
    sub = 2 * half
    specs = [
        _mod_spec(layer, sub, tiles_per_seq),
        _resident_spec((1, D_MODEL), layer, sub),
        _resident_spec((D_MODEL, D_FF), layer, half),
        _resident_spec((D_MODEL, D_FF), layer, half),
        _resident_spec((D_FF, D_MODEL), layer, half),
    ]
    return specs, [mod, nw, w1, w3, w2]


def _mix_ffn_kernel(*refs, n_parts, final):
    x_ref, mix_mod_ref = refs[:2]
    o_refs = refs[2:2 + n_parts]
    wo_refs = refs[2 + n_parts:2 + 2 * n_parts]
    pos = 2 + 2 * n_parts
    y = _dot(o_refs[0][...], wo_refs[0][...])
    for o_ref, w_ref in zip(o_refs[1:], wo_refs[1:]):
        y = y + _dot(o_ref[...], w_ref[...])
    x = x_ref[...] + mix_mod_ref[2:3, :] * y
    x = _swiglu_step(x, *refs[pos:pos + N_FFN_REFS])
    pos += N_FFN_REFS
    if final:
        fn_ref, out_ref = refs[pos:]
        ms = jnp.mean(x * x, axis=-1, keepdims=True)
        x = x * lax.rsqrt(ms + NORM_EPS) * fn_ref[...]
    else:
        out_ref, = refs[pos:]
    out_ref[...] = x


def _mix_ffn(x, mod, nw, w1, w3, w2, layer, seq_len, parts, w_out, w_lead, final_w=None):
    n = x.shape[0]
    tm = MIX_TOKEN_TILE
    tiles_per_seq = None if seq_len is None else seq_len // tm
    row = lambda i: (i, 0)
    in_specs = [pl.BlockSpec((tm, D_MODEL), row), _mod_spec(layer, 1, tiles_per_seq)]
    args = [x, mod]
    in_specs += [pl.BlockSpec((tm, p.shape[1]), row) for p in parts]
    args += list(parts)
    r0 = 0
    for p in parts:
        rows = p.shape[1]
        in_specs.append(pl.BlockSpec((None, rows, D_MODEL), lambda i, b=r0 // rows: (w_lead, b, 0),
                                     pipeline_mode=pl.Buffered(1)))
        args.append(w_out)
        r0 += rows
    ffn_specs, ffn_args = _ffn_specs(mod, nw, w1, w3, w2, layer, 1, tiles_per_seq)
    in_specs += ffn_specs
    args += ffn_args
    if final_w is not None:
        in_specs.append(_resident_spec((1, D_MODEL)))
        args.append(final_w)
    return pl.pallas_call(
        functools.partial(_mix_ffn_kernel, n_parts=len(parts), final=final_w is not None),
        grid=(n // tm,),
        in_specs=in_specs,
        out_specs=pl.BlockSpec((tm, D_MODEL), row),
        out_shape=jax.ShapeDtypeStruct((n, D_MODEL), f32),
        compiler_params=_cparams(1),
    )(*args)


def _lane_iota():
    return lax.broadcasted_iota(jnp.int32, (1, LANES), 1)


def _head_rms(z, low):
    sq = z * z
    s_low = jnp.sum(jnp.where(low, sq, 0.0), axis=-1, keepdims=True)
    s_high = jnp.sum(jnp.where(low, 0.0, sq), axis=-1, keepdims=True)
    return jnp.where(low, s_low, s_high) * (1.0 / HEAD_DIM)


def _rope(z, cos, sin_signed, first_half):
    rot = jnp.where(first_half, pltpu.roll(z, LANES - 16, 1), pltpu.roll(z, 16, 1))
    return z * cos + rot * sin_signed


def _write_cache(ref, z, lanes=slice(None)):
    ref[:, :, lanes] = z.reshape(TOKEN_TILE // SEQ, SEQ, z.shape[1])


def _proj_even_kernel(*refs, rope, ctx, n_alias):
    x_ref = refs[0]
    pos = 1 + N_FFN_REFS
    x = _swiglu_step(x_ref[...], *refs[1:pos])
    mod_ref, nw_ref, w_ref, qn_ref, kn_ref = refs[pos:pos + 5]
    pos += 5
    if rope:
        cos_ref, sin_ref = refs[pos:pos + 2]
        pos += 2
    pos += n_alias
    xo_ref, aq_ref, ak_ref, av_ref, bq_ref, bk_ref, bv_ref = refs[pos:pos + 7]
    if ctx:
        akf_ref, avf_ref, bkf_ref, bvf_ref = refs[pos + 7:pos + 11]
    xo_ref[...] = x

    h = _modulated_norm(x, mod_ref, nw_ref).astype(bf16)
    y = _dot(h, w_ref[...])
    lane = _lane_iota()
    low = lane < HEAD_DIM
    first_half = (lane % 32) < 16
    if rope:
        cos = cos_ref[...]
        sin = sin_ref[...]

    for j in range(A_HEADS // 2):
        z = y[:, j * LANES:(j + 1) * LANES]
        z = z * lax.rsqrt(_head_rms(z, low) + NORM_EPS) * qn_ref[...]
        if rope:
            z = _rope(z, cos, sin, first_half)
        z = z * Q_SCALE
        zr = pltpu.roll(z, HEAD_DIM, 1)
        g = j // 2
        keep = low if g == 0 else jnp.logical_not(low)
        h0 = z if g == 0 else zr
        h1 = zr if g == 0 else z
        aq_ref[:, (2 * j) * LANES:(2 * j + 1) * LANES] = jnp.where(keep, h0, 0.0).astype(bf16)
        aq_ref[:, (2 * j + 1) * LANES:(2 * j + 2) * LANES] = jnp.where(keep, h1, 0.0).astype(bf16)

    o = A_HEADS * HEAD_DIM
    z = y[:, o:o + LANES]
    z = z * lax.rsqrt(_head_rms(z, low) + NORM_EPS) * kn_ref[...]
    if ctx:
        _write_cache(akf_ref, z)
    if rope:
        z = _rope(z, cos, sin, first_half)
    ak_ref[...] = z.astype(bf16)
    o += LANES
    z = y[:, o:o + LANES]
    if ctx:
        _write_cache(avf_ref, z)
    av_ref[...] = z.astype(bf16)
    o += LANES
    nb = B_HEADS * HEAD_DIM
    bq_ref[...] = (y[:, o:o + nb] * Q_SCALE).astype(bf16)
    z = y[:, o + nb:o + 2 * nb]
    if ctx:
        _write_cache(bkf_ref, z)
    bk_ref[...] = z.astype(bf16)
    z = y[:, o + 2 * nb:o + 3 * nb]
    if ctx:
        _write_cache(bvf_ref, z)
    bv_ref[...] = z.astype(bf16)


def _new_cache_outputs(widths, n_layers, slot, prev):
    seqs = TOKEN_TILE // SEQ
    shapes = [jax.ShapeDtypeStruct((BATCH, n_layers, SEQ, w), f32) for w in widths]
    specs = [pl.BlockSpec((seqs, None, SEQ, w), lambda i: (i, slot, 0, 0)) for w in widths]
    if prev is None:
        prev = [jnp.zeros(s.shape, s.dtype) for s in shapes]
    return shapes, specs, [pl.BlockSpec(memory_space=pl.ANY)] * len(widths), list(prev)


def _ffn_proj_even(x, mod, nw, w1, w3, w2, w_in, qn, kn, rope_tabs, layer, ctx, prev_caches=None):
    n = x.shape[0]
    tm = TOKEN_TILE
    e = layer // 2
    rope = rope_tabs is not None
    tiles_per_seq = None if ctx else DEC_SEQ // tm
    row = lambda i: (i, 0)
    ffn_specs, ffn_args = _ffn_specs(mod, nw, w1, w3, w2, layer, 0, tiles_per_seq)
    in_specs = [pl.BlockSpec((tm, D_MODEL), row)] + ffn_specs + [
        _mod_spec(layer, 1, tiles_per_seq),
        _resident_spec((1, D_MODEL), layer, 1),
        _resident_spec((D_MODEL, AB_IN), e),
        _resident_spec((1, LANES), e),
        _resident_spec((1, LANES), e),
    ]
    args = [x] + ffn_args + [mod, nw, w_in, qn, kn]
    if rope:
        pos_spec = pl.BlockSpec((tm, LANES), lambda i: (i % tiles_per_seq, 0))
        in_specs += [pos_spec, pos_spec]
        args += list(rope_tabs)
    widths = [2 * A_HEADS * HEAD_DIM, LANES, LANES, 512, 512, 512]
    out_shape = [jax.ShapeDtypeStruct((n, D_MODEL), f32)] + [jax.ShapeDtypeStruct((n, w), bf16) for w in widths]
    out_specs = [pl.BlockSpec((tm, D_MODEL), row)] + [pl.BlockSpec((tm, w), row) for w in widths]
    aliases = {}
    n_alias = 0
    if ctx:
        shapes, specs, alias_specs, prev = _new_cache_outputs((LANES, LANES, 512, 512), w_in.shape[0], e,
                                                              prev_caches)
        n_alias = len(alias_specs)
        aliases = {len(args) + k: len(out_shape) + k for k in range(n_alias)}
        in_specs += alias_specs
        args += prev
        out_shape += shapes
        out_specs += specs
    return pl.pallas_call(
        functools.partial(_proj_even_kernel, rope=rope, ctx=ctx, n_alias=n_alias),
        grid=(n // tm,),
        in_specs=in_specs,
        out_specs=out_specs,
        out_shape=out_shape,
        input_output_aliases=aliases,
        compiler_params=_cparams(1),
    )(*args)


def _proj_odd_kernel(*refs, rope, ctx, n_alias):
    x_ref = refs[0]
    pos = 1 + N_FFN_REFS
    x = _swiglu_step(x_ref[...], *refs[1:pos])
    mod_ref, nw_ref, w_ref = refs[pos:pos + 3]
    pos += 3
    if rope:
        cos_ref, sin_ref = refs[pos:pos + 2]
        pos += 2
    pos += n_alias
    xo_ref, q_ref, k_ref, v_ref = refs[pos:pos + 4]
    if ctx:
        kf_ref, vf_ref = refs[pos + 4:pos + 6]
    xo_ref[...] = x

    h = _modulated_norm(x, mod_ref, nw_ref).astype(bf16)
    y = _dot(h, w_ref[...])
    lane = _lane_iota()
    first_half = (lane % 32) < 16
    for j in range(C_OUT // LANES):
        zq = y[:, j * LANES:(j + 1) * LANES]
        zk = y[:, C_OUT + j * LANES:C_OUT + (j + 1) * LANES]
        if ctx:
            _write_cache(kf_ref, zk, slice(j * LANES, (j + 1) * LANES))
        if rope:
            zq = _rope(zq, cos_ref[...], sin_ref[...], first_half)
            zk = _rope(zk, cos_ref[...], sin_ref[...], first_half)
        q_ref[:, j * LANES:(j + 1) * LANES] = (zq * Q_SCALE).astype(bf16)
        k_ref[:, j * LANES:(j + 1) * LANES] = zk.astype(bf16)
    zv = y[:, 2 * C_OUT:3 * C_OUT]
    if ctx:
        _write_cache(vf_ref, zv)
    v_ref[...] = zv.astype(bf16)


def _ffn_proj_odd(x, mod, nw, w1, w3, w2, w_in, rope_tabs, layer, ctx, prev_caches=None):
    n = x.shape[0]
    tm = TOKEN_TILE
    o = layer // 2
    rope = rope_tabs is not None
    tiles_per_seq = None if ctx else DEC_SEQ // tm
    row = lambda i: (i, 0)
    ffn_specs, ffn_args = _ffn_specs(mod, nw, w1, w3, w2, layer, 0, tiles_per_seq)
    in_specs = [pl.BlockSpec((tm, D_MODEL), row)] + ffn_specs + [
        _mod_spec(layer, 1, tiles_per_seq),
        _resident_spec((1, D_MODEL), layer, 1),
        _resident_spec((D_MODEL, C_IN), o),
    ]
    args = [x] + ffn_args + [mod, nw, w_in]
    if rope:
        pos_spec = pl.BlockSpec((tm, LANES), lambda i: (i % tiles_per_seq, 0))
        in_specs += [pos_spec, pos_spec]
        args += list(rope_tabs)
    out_shape = [jax.ShapeDtypeStruct((n, D_MODEL), f32)] + [jax.ShapeDtypeStruct((n, C_OUT), bf16)] * 3
    out_specs = [pl.BlockSpec((tm, D_MODEL), row)] + [pl.BlockSpec((tm, C_OUT), row)] * 3
    aliases = {}
    n_alias = 0
    if ctx:
        shapes, specs, alias_specs, prev = _new_cache_outputs((C_OUT, C_OUT), w_in.shape[0], o, prev_caches)
        n_alias = len(alias_specs)
        aliases = {len(args) + k: len(out_shape) + k for k in range(n_alias)}
        in_specs += alias_specs
        args += prev
        out_shape += shapes
        out_specs += specs
    return pl.pallas_call(
        functools.partial(_proj_odd_kernel, rope=rope, ctx=ctx, n_alias=n_alias),
        grid=(n // tm,),
        in_specs=in_specs,
        out_specs=out_specs,
        out_shape=out_shape,
        input_output_aliases=aliases,
        compiler_params=_cparams(1),
    )(*args)


def _lane_tile_reduce(x, op):
    acc = x[:, :LANES]
    for i in range(1, x.shape[1] // LANES):
        acc = op(acc, x[:, i * LANES:(i + 1) * LANES])
    return acc


def _store_scores(s_ref, q, keys, bias=None):
    off = 0
    for i, k in enumerate(keys):
        s = _dot_nt(q, k)
        if i == 0 and bias is not None:
            s = s + bias
        s_ref[:, off:off + k.shape[0]] = s
        off += k.shape[0]
    return _lane_tile_reduce(s_ref[...], jnp.maximum).max(axis=-1, keepdims=True)


def _softmax_values(s_ref, m, values):
    l_acc = None
    o = None
    off = 0
    for load, n in values:
        for c0 in range(0, n, KEY_CHUNK):
            c1 = min(c0 + KEY_CHUNK, n)
            p = jnp.exp2(s_ref[:, off + c0:off + c1] - m)
            lt = _lane_tile_reduce(p, jnp.add)
            l_acc = lt if l_acc is None else l_acc + lt
            t = _dot(p.astype(bf16), load(c0, c1))
            o = t if o is None else o + t
        off += n
    return o, l_acc.sum(axis=-1, keepdims=True)


def _run_units(n_units, s_refs, scores, finish):
    m_next = scores(0, s_refs[0])
    for u in range(n_units):
        m = m_next
        if u + 1 < n_units:
            m_next = scores(u + 1, s_refs[(u + 1) % 2])
        finish(u, s_refs[u % 2], m)


def _score_scratch(rows, n_keys):
    return [pltpu.VMEM((rows, n_keys), f32)] * 2


def _cache_spec(layer, width, col_block=lambda *idx: 0):
    return pl.BlockSpec((None, None, PAST_LEN, width), lambda b, *rest: (b, layer, 0, col_block(b, *rest)))


def _attn_a_kernel(*refs, tq, cache):
    if cache:
        q_ref, k_ref, v_ref, kc_ref, vc_ref, o_ref, s0_ref, s1_ref = refs
    else:
        q_ref, k_ref, v_ref, o_ref, s0_ref, s1_ref = refs
    low = _lane_iota() < HEAD_DIM
    group = A_HEADS // A_KV_HEADS
    seq = k_ref.shape[0]
    n_blocks = tq // A_UNIT_Q

    def unit(u):
        g, qb = divmod(u, n_blocks)
        return g, slice(qb * A_UNIT_Q, (qb + 1) * A_UNIT_Q)

    def scores(u, s_ref):
        g, rows = unit(u)
        q = jnp.concatenate(
            [q_ref[rows, (group * g + i) * LANES:(group * g + i + 1) * LANES] for i in range(group)], axis=0)
        keys = [k_ref[...]]
        if cache:
            keys.append(kc_ref[...].astype(bf16))
        return _store_scores(s_ref, q, keys)

    def finish(u, s_ref, m):
        g, rows = unit(u)
        values = [(lambda a, b: v_ref[a:b, :], seq)]
        if cache:
            values.append((lambda a, b: vc_ref[a:b, :].astype(bf16), PAST_LEN))
        o, l = _softmax_values(s_ref, m, values)
        o = o / l
        o_other = pltpu.roll(o, HEAD_DIM, 1)
        o_low, o_high = (o, o_other) if g == 0 else (o_other, o)
        for jj in range(group // 2):
            even = o_low[(2 * jj) * A_UNIT_Q:(2 * jj + 1) * A_UNIT_Q]
            odd = o_high[(2 * jj + 1) * A_UNIT_Q:(2 * jj + 2) * A_UNIT_Q]
            blk = (group // 2) * g + jj
            o_ref[rows, blk * LANES:(blk + 1) * LANES] = jnp.where(low, even, odd).astype(o_ref.dtype)

    _run_units(A_KV_HEADS * n_blocks, (s0_ref, s1_ref), scores, finish)


def _attn_a(q, k, v, kc, vc, layer, batch, seq, tq):
    cache = kc is not None
    nq = seq // tq
    in_specs = [
        pl.BlockSpec((tq, 2 * A_HEADS * HEAD_DIM), lambda b, j: (b * nq + j, 0)),
        pl.BlockSpec((seq, LANES), lambda b, j: (b, 0)),
        pl.BlockSpec((seq, LANES), lambda b, j: (b, 0)),
    ]
    args = [q, k, v]
    if cache:
        in_specs += [_cache_spec(layer, LANES)] * 2
        args += [kc, vc]
    n_keys = seq + (PAST_LEN if cache else 0)
    return pl.pallas_call(
        functools.partial(_attn_a_kernel, tq=tq, cache=cache),
        grid=(batch, nq),
        in_specs=in_specs,
        out_specs=pl.BlockSpec((tq, A_HEADS * HEAD_DIM), lambda b, j: (b * nq + j, 0)),
        out_shape=jax.ShapeDtypeStruct((batch * seq, A_HEADS * HEAD_DIM), bf16),
        scratch_shapes=_score_scratch(A_HEADS // A_KV_HEADS * A_UNIT_Q, n_keys),
        compiler_params=_cparams(2),
    )(*args)


def _pair_lhs(q, low):
    zero = jnp.zeros_like(q)
    return jnp.concatenate([jnp.where(low, q, zero), jnp.where(low, zero, q)], axis=0)


def _attn_pair_ctx_kernel(q_ref, k_ref, v_ref, o_ref, s0_ref, s1_ref, *, tq):
    low = _lane_iota() < HEAD_DIM
    seq = k_ref.shape[0]

    def scores(hp, s_ref):
        sl = slice(hp * LANES, (hp + 1) * LANES)
        return _store_scores(s_ref, _pair_lhs(q_ref[:, sl], low), [k_ref[:, sl]])

    def finish(hp, s_ref, m):
        sl = slice(hp * LANES, (hp + 1) * LANES)
        o, l = _softmax_values(s_ref, m, [(lambda a, b: v_ref[a:b, sl], seq)])
        o = o / l
        o_ref[:, sl] = jnp.where(low, o[:tq], o[tq:]).astype(o_ref.dtype)

    _run_units(B_HEADS // 2, (s0_ref, s1_ref), scores, finish)


def _attn_pair_ctx(q, k, v, batch, seq):
    spec = pl.BlockSpec((seq, B_HEADS * HEAD_DIM), lambda b: (b, 0))
    return pl.pallas_call(
        functools.partial(_attn_pair_ctx_kernel, tq=seq),
        grid=(batch,),
        in_specs=[spec, spec, spec],
        out_specs=spec,
        out_shape=jax.ShapeDtypeStruct((batch * seq, B_HEADS * HEAD_DIM), bf16),
        scratch_shapes=_score_scratch(2 * seq, seq),
        compiler_params=_cparams(1),
    )(q, k, v)


def _nb_band_start(block):
    return jnp.clip(NB_Q_ROWS * block - NA_ROWS // 2, 0, GRID_ROWS - NB_BAND_ROWS)


def _nb_table_index(block):
    return jnp.where(block < 2, block,
                     jnp.where(block >= NB_BLOCKS - 2, block - (NB_BLOCKS - NB_TABLES), 2))


def _attn_nb_kernel(*refs):
    q_ref, k_ref, v_ref, kc_ref, vc_ref = refs[:5]
    bias_refs = refs[5:5 + NB_STEP_BLOCKS]
    o_ref, s0_ref, s1_ref = refs[5 + NB_STEP_BLOCKS:]
    j = pl.program_id(1)
    low = _lane_iota() < HEAD_DIM
    n_pairs = B_HEADS // 2

    def unit(u):
        qb, hp = divmod(u, n_pairs)
        start = pl.multiple_of(_nb_band_start(j * NB_STEP_BLOCKS + qb) * GRID_W, GRID_W)
        return qb, hp, slice(qb * NB_Q, (qb + 1) * NB_Q), slice(hp * LANES, (hp + 1) * LANES), start

    def scores(u, s_ref):
        qb, hp, rows, sl, start = unit(u)
        q = _pair_lhs(q_ref[rows, sl], low)
        bias = jnp.concatenate([bias_refs[qb][2 * hp], bias_refs[qb][2 * hp + 1]], axis=0)
        keys = [k_ref[pl.ds(start, NB_BAND), sl], kc_ref[:, sl].astype(bf16)]
        return _store_scores(s_ref, q, keys, bias)

    def finish(u, s_ref, m):
        qb, hp, rows, sl, start = unit(u)
        values = [(lambda a, b: v_ref[pl.ds(start + a, b - a), sl], NB_BAND),
                  (lambda a, b: vc_ref[a:b, sl].astype(bf16), PAST_LEN)]
        o, l = _softmax_values(s_ref, m, values)
        o = o / l
        o_ref[rows, sl] = jnp.where(low, o[:NB_Q], o[NB_Q:]).astype(o_ref.dtype)

    _run_units(NB_STEP_BLOCKS * n_pairs, (s0_ref, s1_ref), scores, finish)


def _attn_nb(q, k, v, kc, vc, bias, layer):
    width = B_HEADS * HEAD_DIM
    steps = NB_BLOCKS // NB_STEP_BLOCKS
    tq = NB_STEP_BLOCKS * NB_Q
    bias_specs = [
        pl.BlockSpec((B_HEADS, None, NB_Q, NB_BAND),
                     lambda b, j, qb=qb: (layer, _nb_table_index(j * NB_STEP_BLOCKS + qb), 0, 0))
        for qb in range(NB_STEP_BLOCKS)]
    return pl.pallas_call(
        _attn_nb_kernel,
        grid=(DEC_BATCH, steps),
        in_specs=[
            pl.BlockSpec((tq, width), lambda b, j: (b * steps + j, 0)),
            pl.BlockSpec((DEC_SEQ, width), lambda b, j: (b, 0)),
            pl.BlockSpec((DEC_SEQ, width), lambda b, j: (b, 0)),
            _cache_spec(layer, width),
            _cache_spec(layer, width),
        ] + bias_specs,
        out_specs=pl.BlockSpec((tq, width), lambda b, j: (b * steps + j, 0)),
        out_shape=jax.ShapeDtypeStruct((N_LAT, width), bf16),
        scratch_shapes=_score_scratch(2 * NB_Q, NB_BAND + PAST_LEN),
        compiler_params=_cparams(2),
    )(q, k, v, kc, vc, *([bias] * NB_STEP_BLOCKS))


def _nb_row_offsets():
    dr = np.full((NB_TABLES, NB_Q_ROWS, NB_BAND_ROWS), -1, np.int64)
    for t, block in enumerate(NB_TABLE_BLOCKS):
        band_start = int(np.clip(NB_Q_ROWS * block - NA_ROWS // 2, 0, GRID_ROWS - NB_BAND_ROWS))
        for qr in range(NB_Q_ROWS):
            q_row = block * NB_Q_ROWS + qr
            row_start = int(np.clip(q_row - NA_ROWS // 2, 0, GRID_ROWS - NA_ROWS))
            for kr in range(NB_BAND_ROWS):
                k_row = band_start + kr
                if row_start <= k_row < row_start + NA_ROWS:
                    dr[t, qr, kr] = int(np.clip(k_row - q_row + NA_ROWS - 1, 0, 2 * NA_ROWS - 2))
    return dr


def _nb_bias_kernel(r_ref, o_ref):
    lane = lax.broadcasted_iota(jnp.int32, (GRID_W, LANES), 1)
    q_col = lax.broadcasted_iota(jnp.int32, (GRID_W, LANES), 0)
    k_col = lane % GRID_W
    win_c = min(NA_COLS, GRID_W)
    col_start = jnp.clip(q_col - win_c // 2, 0, GRID_W - win_c)
    col_in = (k_col >= col_start) & (k_col < col_start + win_c)
    low = lane < GRID_W
    masked = jnp.full((GRID_W, LANES), MASK_VALUE, f32)
    tiles = []
    for d in range(RPB_ROWS):
        x = jnp.broadcast_to(r_ref[d:d + 1, :], (GRID_W, LANES))
        y = pltpu.roll(x, LANES - (NA_COLS - 1), 1, stride=1, stride_axis=0)
        tiles.append(jnp.where(col_in, y * LOG2E, MASK_VALUE))
    dr = _nb_row_offsets()
    for t in range(NB_TABLES):
        for qr in range(NB_Q_ROWS):
            for kp in range(NB_BAND_ROWS // 2):
                d0, d1 = int(dr[t, qr, 2 * kp]), int(dr[t, qr, 2 * kp + 1])
                left = tiles[d0] if d0 >= 0 else masked
                right = tiles[d1] if d1 >= 0 else masked
                tile = masked if (d0 < 0 and d1 < 0) else jnp.where(low, left, right)
                o_ref[t, qr * GRID_W:(qr + 1) * GRID_W, kp * LANES:(kp + 1) * LANES] = tile


def _nb_bias_tables(b_rpb):
    heads = b_rpb.shape[0] * B_HEADS
    r = b_rpb.reshape(heads, RPB_ROWS, RPB_COLS)
    half = jnp.pad(r, ((0, 0), (0, 16 - RPB_ROWS), (0, GRID_W - RPB_COLS)))
    r2 = jnp.concatenate([half, half], axis=-1)
    return pl.pallas_call(
        _nb_bias_kernel,
        grid=(heads,),
        in_specs=[pl.BlockSpec((None, 16, LANES), lambda h: (h, 0, 0))],
        out_specs=pl.BlockSpec((None, NB_TABLES, NB_Q, NB_BAND), lambda h: (h, 0, 0, 0)),
        out_shape=jax.ShapeDtypeStruct((heads, NB_TABLES, NB_Q, NB_BAND), f32),
        compiler_params=_cparams(1),
    )(r2)


def _attn_diff_kernel(*refs, tq, cache, lam_init, n_heads):
    if cache:
        lam_ref, sub_ref, q_ref, k_ref, v_ref, kc_ref, vc_ref, o_ref, s0_ref, s1_ref = refs
    else:
        lam_ref, sub_ref, q_ref, k_ref, v_ref, o_ref, s0_ref, s1_ref = refs
    low = _lane_iota() < HEAD_DIM
    seq = k_ref.shape[0]
    lp = lam_ref[...]
    lam = (jnp.exp(jnp.sum(lp[0:1] * lp[1:2], axis=-1, keepdims=True))
           - jnp.exp(jnp.sum(lp[2:3] * lp[3:4], axis=-1, keepdims=True)) + lam_init)

    uq = DIFF_UNIT_Q
    n_blocks = tq // uq

    def unit(u):
        h, qb = divmod(u, n_blocks)
        return slice(qb * uq, (qb + 1) * uq), slice(h * LANES, (h + 1) * LANES)

    def scores(u, s_ref):
        rows, sl = unit(u)
        keys = [k_ref[:, sl]]
        if cache:
            keys.append(kc_ref[:, sl].astype(bf16))
        return _store_scores(s_ref, _pair_lhs(q_ref[rows, sl], low), keys)

    def finish(u, s_ref, m):
        rows, sl = unit(u)
        values = [(lambda a, b: v_ref[a:b, sl], seq)]
        if cache:
            values.append((lambda a, b: vc_ref[a:b, sl].astype(bf16), PAST_LEN))
        o, l = _softmax_values(s_ref, m, values)
        inv = 1.0 / l
        o = o[:uq] * inv[:uq] - o[uq:] * (lam * inv[uq:])
        ms = jnp.mean(o * o, axis=-1, keepdims=True)
        o = o * lax.rsqrt(ms + NORM_EPS) * sub_ref[...] * (1.0 - lam_init)
        o_ref[rows, sl] = o.astype(o_ref.dtype)

    _run_units(n_heads * n_blocks, (s0_ref, s1_ref), scores, finish)


def _attn_diff(lam_p, subln, q, k, v, kc, vc, layer, batch, seq, tq, heads_per_step, lam_init):
    cache = kc is not None
    nq = seq // tq
    gw = heads_per_step * LANES
    in_specs = [
        pl.BlockSpec((None, 4, LANES), lambda b, g, j: (layer, 0, 0)),
        pl.BlockSpec((None, 1, LANES), lambda b, g, j: (layer, 0, 0)),
        pl.BlockSpec((tq, gw), lambda b, g, j: (b * nq + j, g)),
        pl.BlockSpec((seq, gw), lambda b, g, j: (b, g)),
        pl.BlockSpec((seq, gw), lambda b, g, j: (b, g)),
    ]
    args = [lam_p, subln, q, k, v]
    if cache:
        in_specs += [_cache_spec(layer, gw, lambda b, g, j: g)] * 2
        args += [kc, vc]
    n_keys = seq + (PAST_LEN if cache else 0)
    return pl.pallas_call(
        functools.partial(_attn_diff_kernel, tq=tq, cache=cache, lam_init=lam_init, n_heads=heads_per_step),
        grid=(batch, C_HEADS // heads_per_step, nq),
        in_specs=in_specs,
        out_specs=pl.BlockSpec((tq, gw), lambda b, g, j: (b * nq + j, g)),
        out_shape=jax.ShapeDtypeStruct((batch * seq, C_OUT), bf16),
        scratch_shapes=_score_scratch(2 * DIFF_UNIT_Q, n_keys),
        compiler_params=_cparams(3),
    )(*args)


def _rope_tables():
    t = jnp.arange(DEC_SEQ)
    row = (t // GRID_W).astype(f32)
    col = (t % GRID_W).astype(f32)
    n_freq = HEAD_DIM // 4
    inv_freq = ROPE_THETA ** (-jnp.arange(n_freq, dtype=f32) / n_freq)
    ang_r = row[:, None] * inv_freq
    ang_c = col[:, None] * inv_freq
    ang = jnp.concatenate([ang_r, ang_r, ang_c, ang_c], axis=-1)
    cos = jnp.cos(ang)
    sin = jnp.sin(ang)
    sign = jnp.asarray(np.tile(np.repeat(np.array([-1.0, 1.0], np.float32), n_freq), 2))
    return jnp.tile(cos, (1, 2)), jnp.tile(sin * sign, (1, 2))


def kernel(x_prompt, x_sample, cache_a_k, cache_a_v, cache_b_k, cache_b_v, cache_c_k, cache_c_v, c, c_ctx,
           w_mod, b_mod, norm_w, ffn_w1, ffn_w3, ffn_w2, w_in_ab, w_out_ab, a_q_norm, a_k_norm, b_rpb,
           w_in_c, w_out_c, c_lambda, c_subln, final_norm):
    lam_inits = [0.8 - 0.6 * math.exp(-0.3 * l) for l in range(DEPTH)]
    n_even = w_in_ab.shape[0]
    n_odd = w_in_c.shape[0]

    mod = _modulation(c, c_ctx, w_mod, b_mod)
    w1 = ffn_w1.astype(bf16)
    w3 = ffn_w3.astype(bf16)
    w2 = ffn_w2.astype(bf16)
    w_in_ab16 = w_in_ab.astype(bf16)
    w_out_ab16 = w_out_ab.astype(bf16)
    w_in_c16 = w_in_c.astype(bf16)
    w_out_c16 = w_out_c.astype(bf16)
    nw = norm_w.reshape(DEPTH, 3, 1, D_MODEL)
    qn = jnp.tile(a_q_norm, (1, 2)).reshape(n_even, 1, LANES)
    kn = jnp.tile(a_k_norm, (1, 2)).reshape(n_even, 1, LANES)
    lam_p = jnp.pad(c_lambda, ((0, 0), (0, 0), (0, LANES - HEAD_DIM)))
    subln = c_subln.reshape(n_odd, 1, 2 * HEAD_DIM)
    rope_tabs = _rope_tables()
    nb_bias = _nb_bias_tables(b_rpb)

    ca_k = cache_a_k.reshape(DEC_BATCH, n_even, PAST_LEN, A_KV_HEADS * HEAD_DIM)
    ca_v = cache_a_v.reshape(DEC_BATCH, n_even, PAST_LEN, A_KV_HEADS * HEAD_DIM)
    cb_k = cache_b_k.reshape(DEC_BATCH, n_even, PAST_LEN, B_HEADS * HEAD_DIM)
    cb_v = cache_b_v.reshape(DEC_BATCH, n_even, PAST_LEN, B_HEADS * HEAD_DIM)
    cc_k = cache_c_k.reshape(DEC_BATCH, n_odd, PAST_LEN, C_OUT)
    cc_v = cache_c_v.reshape(DEC_BATCH, n_odd, PAST_LEN, C_OUT)

    xs = x_sample.reshape(N_LAT, D_MODEL)
    xp = x_prompt.reshape(N_CTX, D_MODEL)
    fn = final_norm.reshape(1, D_MODEL)
    ffn_w = (mod, nw, w1, w3, w2)
    new_ab = None
    new_c = None

    for l in range(DEPTH):
        fin = fn if l == DEPTH - 1 else None
        if l % 2 == 0:
            e = l // 2
            xs, aq, ak, av, bq, bk, bv = _ffn_proj_even(xs, *ffn_w, w_in_ab16, qn, kn, rope_tabs, l, False)
            oa = _attn_a(aq, ak, av, ca_k, ca_v, e, DEC_BATCH, DEC_SEQ, A_STEP_Q)
            ob = _attn_nb(bq, bk, bv, cb_k, cb_v, nb_bias, e)
            xs = _mix_ffn(xs, *ffn_w, l, DEC_SEQ, (oa, ob), w_out_ab16, e, fin)

            xp, aq, ak, av, bq, bk, bv, *new_ab = _ffn_proj_even(xp, *ffn_w, w_in_ab16, qn, kn, None, l, True,
                                                                 new_ab)
            oa = _attn_a(aq, ak, av, None, None, e, BATCH, SEQ, SEQ)
            ob = _attn_pair_ctx(bq, bk, bv, BATCH, SEQ)
            xp = _mix_ffn(xp, *ffn_w, l, None, (oa, ob), w_out_ab16, e, fin)
        else:
            o = l // 2
            xs, q, k, v = _ffn_proj_odd(xs, *ffn_w, w_in_c16, rope_tabs, l, False)
            oc = _attn_diff(lam_p, subln, q, k, v, cc_k, cc_v, o, DEC_BATCH, DEC_SEQ, DIFF_STEP_Q,
                            DIFF_STEP_HEADS, lam_inits[l])
            xs = _mix_ffn(xs, *ffn_w, l, DEC_SEQ, (oc,), w_out_c16, o, fin)

            xp, q, k, v, *new_c = _ffn_proj_odd(xp, *ffn_w, w_in_c16, None, l, True, new_c)
            oc = _attn_diff(lam_p, subln, q, k, v, None, None, o, BATCH, SEQ, SEQ, C_HEADS, lam_inits[l])
            xp = _mix_ffn(xp, *ffn_w, l, None, (oc,), w_out_c16, o, fin)

    y_sample = xs.reshape(DEC_BATCH, DEC_SEQ, D_MODEL)
    y_prompt = xp.reshape(BATCH, SEQ, D_MODEL)
    ak, av, bk, bv = new_ab
    ck, cv = new_c
    return (y_prompt, y_sample,
            ak.reshape(BATCH, n_even, SEQ, A_KV_HEADS, HEAD_DIM), av.reshape(BATCH, n_even, SEQ, A_KV_HEADS, HEAD_DIM),
            bk.reshape(BATCH, n_even, SEQ, B_HEADS, HEAD_DIM), bv.reshape(BATCH, n_even, SEQ, B_HEADS, HEAD_DIM),
            ck.reshape(BATCH, n_odd, SEQ, C_HEADS, 2 * HEAD_DIM), cv.reshape(BATCH, n_odd, SEQ, C_HEADS, 2 * HEAD_DIM))
```

```python
import functools
import math

import numpy as np
import jax
import jax.numpy as jnp
from jax import lax
from jax.experimental import pallas as pl
from jax.experimental.pallas import tpu as pltpu

D_MODEL = 1024
BATCH = 16
SEQ = 256
DEPTH = 4
DEC_BATCH = 8
DEC_SEQ = 2048
PAST_LEN = 512
GRID_W = 64
GRID_ROWS = DEC_SEQ // GRID_W
HEAD_DIM = 64
A_HEADS = 8
A_KV_HEADS = 2
B_HEADS = 8
C_HEADS = 8
NA_ROWS = 8
NA_COLS = 16
D_FF = 2816
N_MOD = 9
ROPE_THETA = 10000.0
NORM_EPS = 1e-6
MASK_VALUE = -1e30
AB_IN = (A_HEADS + 2 * A_KV_HEADS + 3 * B_HEADS) * HEAD_DIM
C_IN = 3 * C_HEADS * 2 * HEAD_DIM
C_OUT = C_HEADS * 2 * HEAD_DIM

LANES = 128
FF_CHUNK = 256
KEY_CHUNK = 512
MOD_ROWS = 16
CTX_MOD_ROW = DEC_BATCH
VMEM_LIMIT = 52 * 1024 * 1024
TOKEN_TILE = 512

N_LAT = DEC_BATCH * DEC_SEQ
N_CTX = BATCH * SEQ

LOG2E = math.log2(math.e)
Q_SCALE = HEAD_DIM ** -0.5 * LOG2E

A_UNIT_Q = 128
A_STEP_Q = 4 * A_UNIT_Q

DIFF_UNIT_Q = 256
DIFF_STEP_Q = 2 * DIFF_UNIT_Q
DIFF_STEP_HEADS = 4

NB_Q_ROWS = 2
NB_Q = NB_Q_ROWS * GRID_W
NB_BAND_ROWS = 10
NB_BAND = NB_BAND_ROWS * GRID_W
NB_BLOCKS = DEC_SEQ // NB_Q
NB_STEP_BLOCKS = 4
NB_TABLE_BLOCKS = (0, 1, 2, NB_BLOCKS - 2, NB_BLOCKS - 1)
NB_TABLES = len(NB_TABLE_BLOCKS)
RPB_ROWS = 2 * NA_ROWS - 1
RPB_COLS = 2 * NA_COLS - 1

bf16 = jnp.bfloat16
f32 = jnp.float32


def _dot(a, b):
    return jnp.dot(a, b, preferred_element_type=f32)


def _dot_nt(a, b):
    return lax.dot_general(a, b, (((1,), (1,)), ((), ())), preferred_element_type=f32)


def _cparams(n_axes):
    return pltpu.CompilerParams(dimension_semantics=("arbitrary",) * n_axes,
                                vmem_limit_bytes=VMEM_LIMIT)


def _resident_spec(shape, *lead):
    block = (None,) * len(lead) + tuple(shape)
    index = tuple(lead) + (0,) * len(shape)
    return pl.BlockSpec(block, lambda *_: index, pipeline_mode=pl.Buffered(1))


def _modulated_norm(x, mod_ref, nw_ref):
    ms = jnp.mean(x * x, axis=-1, keepdims=True)
    y = x * lax.rsqrt(ms + NORM_EPS) * nw_ref[...]
    return y * (1.0 + mod_ref[1:2, :]) + mod_ref[0:1, :]


def _mod_kernel(c_ref, w_ref, b_ref, o_ref):
    c = c_ref[...]
    s = (c * jax.nn.sigmoid(c)).astype(bf16)
    o_ref[...] = _dot(s, w_ref[...].astype(bf16)) + b_ref[...]


def _modulation(c, c_ctx, w_mod, b_mod):
    cc = jnp.zeros((MOD_ROWS, D_MODEL), f32)
    cc = cc.at[:DEC_BATCH].set(c).at[CTX_MOD_ROW].set(c_ctx)
    out = pl.pallas_call(
        _mod_kernel,
        grid=(DEPTH, N_MOD),
        in_specs=[
            pl.BlockSpec((MOD_ROWS, D_MODEL), lambda l, k: (0, 0)),
            pl.BlockSpec((None, D_MODEL, D_MODEL), lambda l, k: (l, 0, k)),
            pl.BlockSpec((None, None, 1, D_MODEL), lambda l, k: (l, k, 0, 0)),
        ],
        out_specs=pl.BlockSpec((None, None, MOD_ROWS, D_MODEL), lambda l, k: (l, k, 0, 0)),
        out_shape=jax.ShapeDtypeStruct((DEPTH, N_MOD, MOD_ROWS, D_MODEL), f32),
        compiler_params=_cparams(2),
    )(cc, w_mod, b_mod.reshape(DEPTH, N_MOD, 1, D_MODEL))
    return out.reshape(DEPTH, 3, 3, MOD_ROWS, D_MODEL).transpose(0, 1, 3, 2, 4)


def _mod_spec(layer, sub, tiles_per_seq):
    if tiles_per_seq is None:
        return pl.BlockSpec((None, None, None, 3, D_MODEL),
                            lambda i, *_: (layer, sub, CTX_MOD_ROW, 0, 0))
    return pl.BlockSpec((None, None, None, 3, D_MODEL),
                        lambda i, *_: (layer, sub, i // tiles_per_seq, 0, 0))


N_FFN_REFS = 5


def _swiglu_step(x, mod_ref, nw_ref, w1_ref, w3_ref, w2_ref):
    h = _modulated_norm(x, mod_ref, nw_ref).astype(bf16)
    acc = None
    for c0 in range(0, D_FF, FF_CHUNK):
        a1 = _dot(h, w1_ref[:, c0:c0 + FF_CHUNK])
        a3 = _dot(h, w3_ref[:, c0:c0 + FF_CHUNK])
        g = (a1 * jax.nn.sigmoid(a1) * a3).astype(bf16)
        t = _dot(g, w2_ref[c0:c0 + FF_CHUNK, :])
        acc = t if acc is None else acc + t
    return x + 0.5 * mod_ref[2:3, :] * acc


def _ffn_specs(mod, nw, w1, w3, w2, layer, half, tiles_per_seq):
    sub = 2 * half
    specs = [
        _mod_spec(layer, sub, tiles_per_seq),
        _resident_spec((1, D_MODEL), layer, sub),
        _resident_spec((D_MODEL, D_FF), layer, half),
        _resident_spec((D_MODEL, D_FF), layer, half),
        _resident_spec((D_FF, D_MODEL), layer, half),
    ]
    return specs, [mod, nw, w1, w3, w2]


def _mix_ffn_kernel(*refs, n_parts, final):
    x_ref, mix_mod_ref = refs[:2]
    o_refs = refs[2:2 + n_parts]
    wo_refs = refs[2 + n_parts:2 + 2 * n_parts]
    pos = 2 + 2 * n_parts
    y = _dot(o_refs[0][...], wo_refs[0][...])
    for o_ref, w_ref in zip(o_refs[1:], wo_refs[1:]):
        y = y + _dot(o_ref[...], w_ref[...])
    x = x_ref[...] + mix_mod_ref[2:3, :] * y
    x = _swiglu_step(x, *refs[pos:pos + N_FFN_REFS])
    pos += N_FFN_REFS
    if final:
        fn_ref, out_ref = refs[pos:]
        ms = jnp.mean(x * x, axis=-1, keepdims=True)
        x = x * lax.rsqrt(ms + NORM_EPS) * fn_ref[...]
    else:
        out_ref, = refs[pos:]
    out_ref[...] = x


def _mix_ffn(x, mod, nw, w1, w3, w2, layer, seq_len, parts, w_out, w_lead, final_w=None):
    n = x.shape[0]
    tm = TOKEN_TILE
    tiles_per_seq = None if seq_len is None else seq_len // tm
    row = lambda i: (i, 0)
    in_specs = [pl.BlockSpec((tm, D_MODEL), row), _mod_spec(layer, 1, tiles_per_seq)]
    args = [x, mod]
    in_specs += [pl.BlockSpec((tm, p.shape[1]), row) for p in parts]
    args += list(parts)
    r0 = 0
    for p in parts:
        rows = p.shape[1]
        in_specs.append(pl.BlockSpec((None, rows, D_MODEL), lambda i, b=r0 // rows: (w_lead, b, 0),
                                     pipeline_mode=pl.Buffered(1)))
        args.append(w_out)
        r0 += rows
    ffn_specs, ffn_args = _ffn_specs(mod, nw, w1, w3, w2, layer, 1, tiles_per_seq)
    in_specs += ffn_specs
    args += ffn_args
    if final_w is not None:
        in_specs.append(_resident_spec((1, D_MODEL)))
        args.append(final_w)
    return pl.pallas_call(
        functools.partial(_mix_ffn_kernel, n_parts=len(parts), final=final_w is not None),
        grid=(n // tm,),
        in_specs=in_specs,
        out_specs=pl.BlockSpec((tm, D_MODEL), row),
        out_shape=jax.ShapeDtypeStruct((n, D_MODEL), f32),
        compiler_params=_cparams(1),
    )(*args)


def _lane_iota():
    return lax.broadcasted_iota(jnp.int32, (1, LANES), 1)


def _head_rms(z, low):
    sq = z * z
    s_low = jnp.sum(jnp.where(low, sq, 0.0), axis=-1, keepdims=True)
    s_high = jnp.sum(jnp.where(low, 0.0, sq), axis=-1, keepdims=True)
    return jnp.where(low, s_low, s_high) * (1.0 / HEAD_DIM)


def _rope(z, cos, sin_signed, first_half):
    rot = jnp.where(first_half, pltpu.roll(z, LANES - 16, 1), pltpu.roll(z, 16, 1))
    return z * cos + rot * sin_signed


def _write_cache(ref, z, lanes=slice(None)):
    ref[:, :, lanes] = z.reshape(TOKEN_TILE // SEQ, SEQ, z.shape[1])


def _proj_even_kernel(*refs, rope, ctx, n_alias):
    x_ref = refs[0]
    pos = 1 + N_FFN_REFS
    x = _swiglu_step(x_ref[...], *refs[1:pos])
    mod_ref, nw_ref, w_ref, qn_ref, kn_ref = refs[pos:pos + 5]
    pos += 5
    if rope:
        cos_ref, sin_ref = refs[pos:pos + 2]
        pos += 2
    pos += n_alias
    xo_ref, aq_ref, ak_ref, av_ref, bq_ref, bk_ref, bv_ref = refs[pos:pos + 7]
    if ctx:
        akf_ref, avf_ref, bkf_ref, bvf_ref = refs[pos + 7:pos + 11]
    xo_ref[...] = x

    h = _modulated_norm(x, mod_ref, nw_ref).astype(bf16)
    y = _dot(h, w_ref[...])
    lane = _lane_iota()
    low = lane < HEAD_DIM
    first_half = (lane % 32) < 16
    if rope:
        cos = cos_ref[...]
        sin = sin_ref[...]

    for j in range(A_HEADS // 2):
        z = y[:, j * LANES:(j + 1) * LANES]
        z = z * lax.rsqrt(_head_rms(z, low) + NORM_EPS) * qn_ref[...]
        if rope:
            z = _rope(z, cos, sin, first_half)
        z = z * Q_SCALE
        zr = pltpu.roll(z, HEAD_DIM, 1)
        g = j // 2
        keep = low if g == 0 else jnp.logical_not(low)
        h0 = z if g == 0 else zr
        h1 = zr if g == 0 else z
        aq_ref[:, (2 * j) * LANES:(2 * j + 1) * LANES] = jnp.where(keep, h0, 0.0).astype(bf16)
        aq_ref[:, (2 * j + 1) * LANES:(2 * j + 2) * LANES] = jnp.where(keep, h1, 0.0).astype(bf16)

    o = A_HEADS * HEAD_DIM
    z = y[:, o:o + LANES]
    z = z * lax.rsqrt(_head_rms(z, low) + NORM_EPS) * kn_ref[...]
    if ctx:
        _write_cache(akf_ref, z)
    if rope:
        z = _rope(z, cos, sin, first_half)
    ak_ref[...] = z.astype(bf16)
    o += LANES
    z = y[:, o:o + LANES]
    if ctx:
        _write_cache(avf_ref, z)
    av_ref[...] = z.astype(bf16)
    o += LANES
    nb = B_HEADS * HEAD_DIM
    bq_ref[...] = (y[:, o:o + nb] * Q_SCALE).astype(bf16)
    z = y[:, o + nb:o + 2 * nb]
    if ctx:
        _write_cache(bkf_ref, z)
    bk_ref[...] = z.astype(bf16)
    z = y[:, o + 2 * nb:o + 3 * nb]
    if ctx:
        _write_cache(bvf_ref, z)
    bv_ref[...] = z.astype(bf16)


def _new_cache_outputs(widths, n_layers, slot, prev):
    seqs = TOKEN_TILE // SEQ
    shapes = [jax.ShapeDtypeStruct((BATCH, n_layers, SEQ, w), f32) for w in widths]
    specs = [pl.BlockSpec((seqs, None, SEQ, w), lambda i: (i, slot, 0, 0)) for w in widths]
    if prev is None:
        prev = [jnp.zeros(s.shape, s.dtype) for s in shapes]
    return shapes, specs, [pl.BlockSpec(memory_space=pl.ANY)] * len(widths), list(prev)


def _ffn_proj_even(x, mod, nw, w1, w3, w2, w_in, qn, kn, rope_tabs, layer, ctx, prev_caches=None):
    n = x.shape[0]
    tm = TOKEN_TILE
    e = layer // 2
    rope = rope_tabs is not None
    tiles_per_seq = None if ctx else DEC_SEQ // tm
    row = lambda i: (i, 0)
    ffn_specs, ffn_args = _ffn_specs(mod, nw, w1, w3, w2, layer, 0, tiles_per_seq)
    in_specs = [pl.BlockSpec((tm, D_MODEL), row)] + ffn_specs + [
        _mod_spec(layer, 1, tiles_per_seq),
        _resident_spec((1, D_MODEL), layer, 1),
        _resident_spec((D_MODEL, AB_IN), e),
        _resident_spec((1, LANES), e),
        _resident_spec((1, LANES), e),
    ]
    args = [x] + ffn_args + [mod, nw, w_in, qn, kn]
    if rope:
        pos_spec = pl.BlockSpec((tm, LANES), lambda i: (i % tiles_per_seq, 0))
        in_specs += [pos_spec, pos_spec]
        args += list(rope_tabs)
    b_width = B_HEADS * HEAD_DIM
    widths = [2 * A_HEADS * HEAD_DIM, LANES, LANES, b_width, b_width, b_width]
    out_shape = [jax.ShapeDtypeStruct((n, D_MODEL), f32)] + [jax.ShapeDtypeStruct((n, w), bf16) for w in widths]
    out_specs = [pl.BlockSpec((tm, D_MODEL), row)] + [pl.BlockSpec((tm, w), row) for w in widths]
    aliases = {}
    n_alias = 0
    if ctx:
        shapes, specs, alias_specs, prev = _new_cache_outputs((LANES, LANES, b_width, b_width), w_in.shape[0], e,
                                                              prev_caches)
        n_alias = len(alias_specs)
        aliases = {len(args) + k: len(out_shape) + k for k in range(n_alias)}
        in_specs += alias_specs
        args += prev
        out_shape += shapes
        out_specs += specs
    return pl.pallas_call(
        functools.partial(_proj_even_kernel, rope=rope, ctx=ctx, n_alias=n_alias),
        grid=(n // tm,),
        in_specs=in_specs,
        out_specs=out_specs,
        out_shape=out_shape,
        input_output_aliases=aliases,
        compiler_params=_cparams(1),
    )(*args)


def _proj_odd_kernel(*refs, rope, ctx, n_alias):
    x_ref = refs[0]
    pos = 1 + N_FFN_REFS
    x = _swiglu_step(x_ref[...], *refs[1:pos])
    mod_ref, nw_ref, w_ref = refs[pos:pos + 3]
    pos += 3
    if rope:
        cos_ref, sin_ref = refs[pos:pos + 2]
        pos += 2
    pos += n_alias
    xo_ref, q_ref, k_ref, v_ref = refs[pos:pos + 4]
    if ctx:
        kf_ref, vf_ref = refs[pos + 4:pos + 6]
    xo_ref[...] = x

    h = _modulated_norm(x, mod_ref, nw_ref).astype(bf16)
    y = _dot(h, w_ref[...])
    lane = _lane_iota()
    first_half = (lane % 32) < 16
    for j in range(C_OUT // LANES):
        zq = y[:, j * LANES:(j + 1) * LANES]
        zk = y[:, C_OUT + j * LANES:C_OUT + (j + 1) * LANES]
        if ctx:
            _write_cache(kf_ref, zk, slice(j * LANES, (j + 1) * LANES))
        if rope:
            zq = _rope(zq, cos_ref[...], sin_ref[...], first_half)
            zk = _rope(zk, cos_ref[...], sin_ref[...], first_half)
        q_ref[:, j * LANES:(j + 1) * LANES] = (zq * Q_SCALE).astype(bf16)
        k_ref[:, j * LANES:(j + 1) * LANES] = zk.astype(bf16)
    zv = y[:, 2 * C_OUT:3 * C_OUT]
    if ctx:
        _write_cache(vf_ref, zv)
    v_ref[...] = zv.astype(bf16)


def _ffn_proj_odd(x, mod, nw, w1, w3, w2, w_in, rope_tabs, layer, ctx, prev_caches=None):
    n = x.shape[0]
    tm = TOKEN_TILE
    o = layer // 2
    rope = rope_tabs is not None
    tiles_per_seq = None if ctx else DEC_SEQ // tm
    row = lambda i: (i, 0)
    ffn_specs, ffn_args = _ffn_specs(mod, nw, w1, w3, w2, layer, 0, tiles_per_seq)
    in_specs = [pl.BlockSpec((tm, D_MODEL), row)] + ffn_specs + [
        _mod_spec(layer, 1, tiles_per_seq),
        _resident_spec((1, D_MODEL), layer, 1),
        _resident_spec((D_MODEL, C_IN), o),
    ]
    args = [x] + ffn_args + [mod, nw, w_in]
    if rope:
        pos_spec = pl.BlockSpec((tm, LANES), lambda i: (i % tiles_per_seq, 0))
        in_specs += [pos_spec, pos_spec]
        args += list(rope_tabs)
    out_shape = [jax.ShapeDtypeStruct((n, D_MODEL), f32)] + [jax.ShapeDtypeStruct((n, C_OUT), bf16)] * 3
    out_specs = [pl.BlockSpec((tm, D_MODEL), row)] + [pl.BlockSpec((tm, C_OUT), row)] * 3
    aliases = {}
    n_alias = 0
    if ctx:
        shapes, specs, alias_specs, prev = _new_cache_outputs((C_OUT, C_OUT), w_in.shape[0], o, prev_caches)
        n_alias = len(alias_specs)
        aliases = {len(args) + k: len(out_shape) + k for k in range(n_alias)}
        in_specs += alias_specs
        args += prev
        out_shape += shapes
        out_specs += specs
    return pl.pallas_call(
        functools.partial(_proj_odd_kernel, rope=rope, ctx=ctx, n_alias=n_alias),
        grid=(n // tm,),
        in_specs=in_specs,
        out_specs=out_specs,
        out_shape=out_shape,
        input_output_aliases=aliases,
        compiler_params=_cparams(1),
    )(*args)


def _lane_tile_reduce(x, op):
    acc = x[:, :LANES]
    for i in range(1, x.shape[1] // LANES):
        acc = op(acc, x[:, i * LANES:(i + 1) * LANES])
    return acc


def _store_scores(s_ref, q, keys, bias=None):
    off = 0
    for i, k in enumerate(keys):
        s = _dot_nt(q, k)
        if i == 0 and bias is not None:
            s = s + bias
        s_ref[:, off:off + k.shape[0]] = s
        off += k.shape[0]
    return _lane_tile_reduce(s_ref[...], jnp.maximum).max(axis=-1, keepdims=True)


def _softmax_values(s_ref, m, values):
    l_acc = None
    o = None
    off = 0
    for load, n in values:
        for c0 in range(0, n, KEY_CHUNK):
            c1 = min(c0 + KEY_CHUNK, n)
            p = jnp.exp2(s_ref[:, off + c0:off + c1] - m)
            lt = _lane_tile_reduce(p, jnp.add)
            l_acc = lt if l_acc is None else l_acc + lt
            t = _dot(p.astype(bf16), load(c0, c1))
            o = t if o is None else o + t
        off += n
    return o, l_acc.sum(axis=-1, keepdims=True)


def _run_units(n_units, s_refs, scores, finish):
    m_next = scores(0, s_refs[0])
    for u in range(n_units):
        m = m_next
        if u + 1 < n_units:
            m_next = scores(u + 1, s_refs[(u + 1) % 2])
        finish(u, s_refs[u % 2], m)


def _score_scratch(rows, n_keys):
    return [pltpu.VMEM((rows, n_keys), f32)] * 2


def _cache_spec(layer, width, col_block=lambda *idx: 0):
    return pl.BlockSpec((None, None, PAST_LEN, width), lambda b, *rest: (b, layer, 0, col_block(b, *rest)))


def _attn_a_kernel(*refs, tq, cache):
    if cache:
        q_ref, k_ref, v_ref, kc_ref, vc_ref, o_ref, s0_ref, s1_ref = refs
    else:
        q_ref, k_ref, v_ref, o_ref, s0_ref, s1_ref = refs
    low = _lane_iota() < HEAD_DIM
    group = A_HEADS // A_KV_HEADS
    seq = k_ref.shape[0]
    n_blocks = tq // A_UNIT_Q

    def unit(u):
        g, qb = divmod(u, n_blocks)
        return g, slice(qb * A_UNIT_Q, (qb + 1) * A_UNIT_Q)

    def scores(u, s_ref):
        g, rows = unit(u)
        q = jnp.concatenate(
            [q_ref[rows, (group * g + i) * LANES:(group * g + i + 1) * LANES] for i in range(group)], axis=0)
        keys = [k_ref[...]]
        if cache:
            keys.append(kc_ref[...].astype(bf16))
        return _store_scores(s_ref, q, keys)

    def finish(u, s_ref, m):
        g, rows = unit(u)
        values = [(lambda a, b: v_ref[a:b, :], seq)]
        if cache:
            values.append((lambda a, b: vc_ref[a:b, :].astype(bf16), PAST_LEN))
        o, l = _softmax_values(s_ref, m, values)
        o = o / l
        o_other = pltpu.roll(o, HEAD_DIM, 1)
        o_low, o_high = (o, o_other) if g == 0 else (o_other, o)
        for jj in range(group // 2):
            even = o_low[(2 * jj) * A_UNIT_Q:(2 * jj + 1) * A_UNIT_Q]
            odd = o_high[(2 * jj + 1) * A_UNIT_Q:(2 * jj + 2) * A_UNIT_Q]
            blk = (group // 2) * g + jj
            o_ref[rows, blk * LANES:(blk + 1) * LANES] = jnp.where(low, even, odd).astype(o_ref.dtype)

    _run_units(A_KV_HEADS * n_blocks, (s0_ref, s1_ref), scores, finish)


def _attn_a(q, k, v, kc, vc, layer, batch, seq, tq):
    cache = kc is not None
    nq = seq // tq
    in_specs = [
        pl.BlockSpec((tq, 2 * A_HEADS * HEAD_DIM), lambda b, j: (b * nq + j, 0)),
        pl.BlockSpec((seq, LANES), lambda b, j: (b, 0)),
        pl.BlockSpec((seq, LANES), lambda b, j: (b, 0)),
    ]
    args = [q, k, v]
    if cache:
        in_specs += [_cache_spec(layer, LANES)] * 2
        args += [kc, vc]
    n_keys = seq + (PAST_LEN if cache else 0)
    return pl.pallas_call(
        functools.partial(_attn_a_kernel, tq=tq, cache=cache),
        grid=(batch, nq),
        in_specs=in_specs,
        out_specs=pl.BlockSpec((tq, A_HEADS * HEAD_DIM), lambda b, j: (b * nq + j, 0)),
        out_shape=jax.ShapeDtypeStruct((batch * seq, A_HEADS * HEAD_DIM), bf16),
        scratch_shapes=_score_scratch(A_HEADS // A_KV_HEADS * A_UNIT_Q, n_keys),
        compiler_params=_cparams(2),
    )(*args)


def _pair_lhs(q, low):
    zero = jnp.zeros_like(q)
    return jnp.concatenate([jnp.where(low, q, zero), jnp.where(low, zero, q)], axis=0)


def _attn_pair_ctx_kernel(q_ref, k_ref, v_ref, o_ref, s0_ref, s1_ref, *, tq):
    low = _lane_iota() < HEAD_DIM
    seq = k_ref.shape[0]

    def scores(hp, s_ref):
        sl = slice(hp * LANES, (hp + 1) * LANES)
        return _store_scores(s_ref, _pair_lhs(q_ref[:, sl], low), [k_ref[:, sl]])

    def finish(hp, s_ref, m):
        sl = slice(hp * LANES, (hp + 1) * LANES)
        o, l = _softmax_values(s_ref, m, [(lambda a, b: v_ref[a:b, sl], seq)])
        o = o / l
        o_ref[:, sl] = jnp.where(low, o[:tq], o[tq:]).astype(o_ref.dtype)

    _run_units(B_HEADS // 2, (s0_ref, s1_ref), scores, finish)


def _attn_pair_ctx(q, k, v, batch, seq):
    spec = pl.BlockSpec((seq, B_HEADS * HEAD_DIM), lambda b: (b, 0))
    return pl.pallas_call(
        functools.partial(_attn_pair_ctx_kernel, tq=seq),
        grid=(batch,),
        in_specs=[spec, spec, spec],
        out_specs=spec,
        out_shape=jax.ShapeDtypeStruct((batch * seq, B_HEADS * HEAD_DIM), bf16),
        scratch_shapes=_score_scratch(2 * seq, seq),
        compiler_params=_cparams(1),
    )(q, k, v)


def _nb_band_start(block):
    return jnp.clip(NB_Q_ROWS * block - NA_ROWS // 2, 0, GRID_ROWS - NB_BAND_ROWS)


def _nb_table_index(block):
    return jnp.where(block < 2, block,
                     jnp.where(block >= NB_BLOCKS - 2, block - (NB_BLOCKS - NB_TABLES), 2))


def _attn_nb_kernel(*refs):
    q_ref, k_ref, v_ref, kc_ref, vc_ref = refs[:5]
    bias_refs = refs[5:5 + NB_STEP_BLOCKS]
    o_ref, s0_ref, s1_ref = refs[5 + NB_STEP_BLOCKS:]
    j = pl.program_id(1)
    low = _lane_iota() < HEAD_DIM
    n_pairs = B_HEADS // 2

    def unit(u):
        qb, hp = divmod(u, n_pairs)
        start = pl.multiple_of(_nb_band_start(j * NB_STEP_BLOCKS + qb) * GRID_W, GRID_W)
        return qb, hp, slice(qb * NB_Q, (qb + 1) * NB_Q), slice(hp * LANES, (hp + 1) * LANES), start

    def scores(u, s_ref):
        qb, hp, rows, sl, start = unit(u)
        q = _pair_lhs(q_ref[rows, sl], low)
        bias = jnp.concatenate([bias_refs[qb][2 * hp], bias_refs[qb][2 * hp + 1]], axis=0)
        keys = [k_ref[pl.ds(start, NB_BAND), sl], kc_ref[:, sl].astype(bf16)]
        return _store_scores(s_ref, q, keys, bias)

    def finish(u, s_ref, m):
        qb, hp, rows, sl, start = unit(u)
        values = [(lambda a, b: v_ref[pl.ds(start + a, b - a), sl], NB_BAND),
                  (lambda a, b: vc_ref[a:b, sl].astype(bf16), PAST_LEN)]
        o, l = _softmax_values(s_ref, m, values)
        o = o / l
        o_ref[rows, sl] = jnp.where(low, o[:NB_Q], o[NB_Q:]).astype(o_ref.dtype)

    _run_units(NB_STEP_BLOCKS * n_pairs, (s0_ref, s1_ref), scores, finish)


def _attn_nb(q, k, v, kc, vc, bias, layer):
    width = B_HEADS * HEAD_DIM
    steps = NB_BLOCKS // NB_STEP_BLOCKS
    tq = NB_STEP_BLOCKS * NB_Q
    bias_specs = [
        pl.BlockSpec((B_HEADS, None, NB_Q, NB_BAND),
                     lambda b, j, qb=qb: (layer, _nb_table_index(j * NB_STEP_BLOCKS + qb), 0, 0))
        for qb in range(NB_STEP_BLOCKS)]
    return pl.pallas_call(
        _attn_nb_kernel,
        grid=(DEC_BATCH, steps),
        in_specs=[
            pl.BlockSpec((tq, width), lambda b, j: (b * steps + j, 0)),
            pl.BlockSpec((DEC_SEQ, width), lambda b, j: (b, 0)),
            pl.BlockSpec((DEC_SEQ, width), lambda b, j: (b, 0)),
            _cache_spec(layer, width),
            _cache_spec(layer, width),
        ] + bias_specs,
        out_specs=pl.BlockSpec((tq, width), lambda b, j: (b * steps + j, 0)),
        out_shape=jax.ShapeDtypeStruct((N_LAT, width), bf16),
        scratch_shapes=_score_scratch(2 * NB_Q, NB_BAND + PAST_LEN),
        compiler_params=_cparams(2),
    )(q, k, v, kc, vc, *([bias] * NB_STEP_BLOCKS))


def _nb_row_offsets():
    dr = np.full((NB_TABLES, NB_Q_ROWS, NB_BAND_ROWS), -1, np.int64)
    for t, block in enumerate(NB_TABLE_BLOCKS):
        band_start = int(np.clip(NB_Q_ROWS * block - NA_ROWS // 2, 0, GRID_ROWS - NB_BAND_ROWS))
        for qr in range(NB_Q_ROWS):
            q_row = block * NB_Q_ROWS + qr
            row_start = int(np.clip(q_row - NA_ROWS // 2, 0, GRID_ROWS - NA_ROWS))
            for kr in range(NB_BAND_ROWS):
                k_row = band_start + kr
                if row_start <= k_row < row_start + NA_ROWS:
                    dr[t, qr, kr] = int(np.clip(k_row - q_row + NA_ROWS - 1, 0, 2 * NA_ROWS - 2))
    return dr


def _nb_bias_kernel(r_ref, o_ref):
    lane = lax.broadcasted_iota(jnp.int32, (GRID_W, LANES), 1)
    q_col = lax.broadcasted_iota(jnp.int32, (GRID_W, LANES), 0)
    k_col = lane % GRID_W
    win_c = min(NA_COLS, GRID_W)
    col_start = jnp.clip(q_col - win_c // 2, 0, GRID_W - win_c)
    col_in = (k_col >= col_start) & (k_col < col_start + win_c)
    low = lane < GRID_W
    masked = jnp.full((GRID_W, LANES), MASK_VALUE, f32)
    tiles = []
    for d in range(RPB_ROWS):
        x = jnp.broadcast_to(r_ref[d:d + 1, :], (GRID_W, LANES))
        y = pltpu.roll(x, LANES - (NA_COLS - 1), 1, stride=1, stride_axis=0)
        tiles.append(jnp.where(col_in, y * LOG2E, MASK_VALUE))
    dr = _nb_row_offsets()
    for t in range(NB_TABLES):
        for qr in range(NB_Q_ROWS):
            for kp in range(NB_BAND_ROWS // 2):
                d0, d1 = int(dr[t, qr, 2 * kp]), int(dr[t, qr, 2 * kp + 1])
                left = tiles[d0] if d0 >= 0 else masked
                right = tiles[d1] if d1 >= 0 else masked
                tile = masked if (d0 < 0 and d1 < 0) else jnp.where(low, left, right)
                o_ref[t, qr * GRID_W:(qr + 1) * GRID_W, kp * LANES:(kp + 1) * LANES] = tile


def _nb_bias_tables(b_rpb):
    heads = b_rpb.shape[0] * B_HEADS
    r = b_rpb.reshape(heads, RPB_ROWS, RPB_COLS)
    half = jnp.pad(r, ((0, 0), (0, 16 - RPB_ROWS), (0, GRID_W - RPB_COLS)))
    r2 = jnp.concatenate([half, half], axis=-1)
    return pl.pallas_call(
        _nb_bias_kernel,
        grid=(heads,),
        in_specs=[pl.BlockSpec((None, 16, LANES), lambda h: (h, 0, 0))],
        out_specs=pl.BlockSpec((None, NB_TABLES, NB_Q, NB_BAND), lambda h: (h, 0, 0, 0)),
        out_shape=jax.ShapeDtypeStruct((heads, NB_TABLES, NB_Q, NB_BAND), f32),
        compiler_params=_cparams(1),
    )(r2)


def _attn_diff_kernel(*refs, tq, cache, lam_init, n_heads):
    if cache:
        lam_ref, sub_ref, q_ref, k_ref, v_ref, kc_ref, vc_ref, o_ref, s0_ref, s1_ref = refs
    else:
        lam_ref, sub_ref, q_ref, k_ref, v_ref, o_ref, s0_ref, s1_ref = refs
    low = _lane_iota() < HEAD_DIM
    seq = k_ref.shape[0]
    lp = lam_ref[...]
    lam = (jnp.exp(jnp.sum(lp[0:1] * lp[1:2], axis=-1, keepdims=True))
           - jnp.exp(jnp.sum(lp[2:3] * lp[3:4], axis=-1, keepdims=True)) + lam_init)

    uq = DIFF_UNIT_Q
    n_blocks = tq // uq

    def unit(u):
        h, qb = divmod(u, n_blocks)
        return slice(qb * uq, (qb + 1) * uq), slice(h * LANES, (h + 1) * LANES)

    def scores(u, s_ref):
        rows, sl = unit(u)
        keys = [k_ref[:, sl]]
        if cache:
            keys.append(kc_ref[:, sl].astype(bf16))
        return _store_scores(s_ref, _pair_lhs(q_ref[rows, sl], low), keys)

    def finish(u, s_ref, m):
        rows, sl = unit(u)
        values = [(lambda a, b: v_ref[a:b, sl], seq)]
        if cache:
            values.append((lambda a, b: vc_ref[a:b, sl].astype(bf16), PAST_LEN))
        o, l = _softmax_values(s_ref, m, values)
        inv = 1.0 / l
        o = o[:uq] * inv[:uq] - o[uq:] * (lam * inv[uq:])
        ms = jnp.mean(o * o, axis=-1, keepdims=True)
        o = o * lax.rsqrt(ms + NORM_EPS) * sub_ref[...] * (1.0 - lam_init)
        o_ref[rows, sl] = o.astype(o_ref.dtype)

    _run_units(n_heads * n_blocks, (s0_ref, s1_ref), scores, finish)


def _attn_diff(lam_p, subln, q, k, v, kc, vc, layer, batch, seq, tq, heads_per_step, lam_init):
    cache = kc is not None
    nq = seq // tq
    gw = heads_per_step * LANES
    in_specs = [
        pl.BlockSpec((None, 4, LANES), lambda b, g, j: (layer, 0, 0)),
        pl.BlockSpec((None, 1, LANES), lambda b, g, j: (layer, 0, 0)),
        pl.BlockSpec((tq, gw), lambda b, g, j: (b * nq + j, g)),
        pl.BlockSpec((seq, gw), lambda b, g, j: (b, g)),
        pl.BlockSpec((seq, gw), lambda b, g, j: (b, g)),
    ]
    args = [lam_p, subln, q, k, v]
    if cache:
        in_specs += [_cache_spec(layer, gw, lambda b, g, j: g)] * 2
        args += [kc, vc]
    n_keys = seq + (PAST_LEN if cache else 0)
    return pl.pallas_call(
        functools.partial(_attn_diff_kernel, tq=tq, cache=cache, lam_init=lam_init, n_heads=heads_per_step),
        grid=(batch, C_HEADS // heads_per_step, nq),
        in_specs=in_specs,
        out_specs=pl.BlockSpec((tq, gw), lambda b, g, j: (b * nq + j, g)),
        out_shape=jax.ShapeDtypeStruct((batch * seq, C_OUT), bf16),
        scratch_shapes=_score_scratch(2 * DIFF_UNIT_Q, n_keys),
        compiler_params=_cparams(3),
    )(*args)


def _rope_tables():
    t = jnp.arange(DEC_SEQ)
    row = (t // GRID_W).astype(f32)
    col = (t % GRID_W).astype(f32)
    n_freq = HEAD_DIM // 4
    inv_freq = ROPE_THETA ** (-jnp.arange(n_freq, dtype=f32) / n_freq)
    ang_r = row[:, None] * inv_freq
    ang_c = col[:, None] * inv_freq
    ang = jnp.concatenate([ang_r, ang_r, ang_c, ang_c], axis=-1)
    cos = jnp.cos(ang)
    sin = jnp.sin(ang)
    sign = jnp.asarray(np.tile(np.repeat(np.array([-1.0, 1.0], np.float32), n_freq), 2))
    return jnp.tile(cos, (1, 2)), jnp.tile(sin * sign, (1, 2))


def kernel(x_prompt, x_sample, cache_a_k, cache_a_v, cache_b_k, cache_b_v, cache_c_k, cache_c_v, c, c_ctx,
           w_mod, b_mod, norm_w, ffn_w1, ffn_w3, ffn_w2, w_in_ab, w_out_ab, a_q_norm, a_k_norm, b_rpb,
           w_in_c, w_out_c, c_lambda, c_subln, final_norm):
    lam_inits = [0.8 - 0.6 * math.exp(-0.3 * l) for l in range(DEPTH)]
    n_even = w_in_ab.shape[0]
    n_odd = w_in_c.shape[0]

    mod = _modulation(c, c_ctx, w_mod, b_mod)
    w1 = ffn_w1.astype(bf16)
    w3 = ffn_w3.astype(bf16)
    w2 = ffn_w2.astype(bf16)
    w_in_ab16 = w_in_ab.astype(bf16)
    w_out_ab16 = w_out_ab.astype(bf16)
    w_in_c16 = w_in_c.astype(bf16)
    w_out_c16 = w_out_c.astype(bf16)
    nw = norm_w.reshape(DEPTH, 3, 1, D_MODEL)
    qn = jnp.tile(a_q_norm, (1, 2)).reshape(n_even, 1, LANES)
    kn = jnp.tile(a_k_norm, (1, 2)).reshape(n_even, 1, LANES)
    lam_p = jnp.pad(c_lambda, ((0, 0), (0, 0), (0, LANES - HEAD_DIM)))
    subln = c_subln.reshape(n_odd, 1, 2 * HEAD_DIM)
    rope_tabs = _rope_tables()
    nb_bias = _nb_bias_tables(b_rpb)

    ca_k = cache_a_k.reshape(DEC_BATCH, n_even, PAST_LEN, A_KV_HEADS * HEAD_DIM)
    ca_v = cache_a_v.reshape(DEC_BATCH, n_even, PAST_LEN, A_KV_HEADS * HEAD_DIM)
    cb_k = cache_b_k.reshape(DEC_BATCH, n_even, PAST_LEN, B_HEADS * HEAD_DIM)
    cb_v = cache_b_v.reshape(DEC_BATCH, n_even, PAST_LEN, B_HEADS * HEAD_DIM)
    cc_k = cache_c_k.reshape(DEC_BATCH, n_odd, PAST_LEN, C_OUT)
    cc_v = cache_c_v.reshape(DEC_BATCH, n_odd, PAST_LEN, C_OUT)

    xs = x_sample.reshape(N_LAT, D_MODEL)
    xp = x_prompt.reshape(N_CTX, D_MODEL)
    fn = final_norm.reshape(1, D_MODEL)
    ffn_w = (mod, nw, w1, w3, w2)
    new_ab = None
    new_c = None

    for l in range(DEPTH):
        fin = fn if l == DEPTH - 1 else None
        if l % 2 == 0:
            e = l // 2
            xs, aq, ak, av, bq, bk, bv = _ffn_proj_even(xs, *ffn_w, w_in_ab16, qn, kn, rope_tabs, l, False)
            oa = _attn_a(aq, ak, av, ca_k, ca_v, e, DEC_BATCH, DEC_SEQ, A_STEP_Q)
            ob = _attn_nb(bq, bk, bv, cb_k, cb_v, nb_bias, e)
            xs = _mix_ffn(xs, *ffn_w, l, DEC_SEQ, (oa, ob), w_out_ab16, e, fin)

            xp, aq, ak, av, bq, bk, bv, *new_ab = _ffn_proj_even(xp, *ffn_w, w_in_ab16, qn, kn, None, l, True,
                                                                 new_ab)
            oa = _attn_a(aq, ak, av, None, None, e, BATCH, SEQ, SEQ)
            ob = _attn_pair_ctx(bq, bk, bv, BATCH, SEQ)
            xp = _mix_ffn(xp, *ffn_w, l, None, (oa, ob), w_out_ab16, e, fin)
        else:
            o = l // 2
            xs, q, k, v = _ffn_proj_odd(xs, *ffn_w, w_in_c16, rope_tabs, l, False)
            oc = _attn_diff(lam_p, subln, q, k, v, cc_k, cc_v, o, DEC_BATCH, DEC_SEQ, DIFF_STEP_Q,
                            DIFF_STEP_HEADS, lam_inits[l])
            xs = _mix_ffn(xs, *ffn_w, l, DEC_SEQ, (oc,), w_out_c16, o, fin)

            xp, q, k, v, *new_c = _ffn_proj_odd(xp, *ffn_w, w_in_c16, None, l, True, new_c)
            oc = _attn_diff(lam_p, subln, q, k, v, None, None, o, BATCH, SEQ, SEQ, C_HEADS, lam_inits[l])
            xp = _mix_ffn(xp, *ffn_w, l, None, (oc,), w_out_c16, o, fin)

    y_sample = xs.reshape(DEC_BATCH, DEC_SEQ, D_MODEL)
    y_prompt = xp.reshape(BATCH, SEQ, D_MODEL)
    ak, av, bk, bv = new_ab
    ck, cv = new_c
    return (y_prompt, y_sample,
            ak.reshape(BATCH, n_even, SEQ, A_KV_HEADS, HEAD_DIM), av.reshape(BATCH, n_even, SEQ, A_KV_HEADS, HEAD_DIM),
            bk.reshape(BATCH, n_even, SEQ, B_HEADS, HEAD_DIM), bv.reshape(BATCH, n_even, SEQ, B_HEADS, HEAD_DIM),
            ck.reshape(BATCH, n_odd, SEQ, C_HEADS, 2 * HEAD_DIM), cv.reshape(BATCH, n_odd, SEQ, C_HEADS, 2 * HEAD_DIM))
```
